```python
import jax
import jax.numpy as jnp
from jax import lax
import numpy as np

D_MODEL = 2048
BATCH = 1
SEQ = 16384
DEPTH = 4
DEC_BATCH = 32
DEC_SEQ = 16
PAST_LEN = 4096

CHUNK = 64
CONV_DIM = D_MODEL // 2
CONV_WIDTH = 31
HEAD_DIM = 64
N_Q_HEADS = (D_MODEL // 2) // HEAD_DIM
N_KV_HEADS = 4
GROUP = N_Q_HEADS // N_KV_HEADS
ATTN_DIM = N_Q_HEADS * HEAD_DIM
KV_DIM = N_KV_HEADS * HEAD_DIM
WINDOW = 128
WIN_CHUNKS = WINDOW // CHUNK
ROPE_DIM = HEAD_DIM // 4
ROPE_THETA = 500000.0
D_FF = ((8 * D_MODEL + 3 * 256 - 1) // (3 * 256)) * 256
EPS = 1e-6
SPLIT_SIZES = (CONV_DIM, CONV_DIM, ATTN_DIM, KV_DIM, KV_DIM, D_MODEL, D_MODEL)
IN_DIM = sum(SPLIT_SIZES)

kernel_name = 'chunk_stream_conv_swa_hybrid'


def _rms_norm(x, g):
    xf = x.astype(jnp.float32)
    y = xf * lax.rsqrt(jnp.mean(xf * xf, axis=-1, keepdims=True) + EPS)
    return (y * g.astype(jnp.float32)).astype(x.dtype)


def _layer_norm(x, g, b):
    xf = x.astype(jnp.float32)
    xc = xf - jnp.mean(xf, axis=-1, keepdims=True)
    var = jnp.mean(xc * xc, axis=-1, keepdims=True)
    y = xc * lax.rsqrt(var + EPS) * g.astype(jnp.float32) + b.astype(jnp.float32)
    return y.astype(x.dtype)


def _partial_rope(x, pos):
    half = ROPE_DIM // 2
    inv_freq = ROPE_THETA ** (-jnp.arange(half, dtype=jnp.float32) * 2.0 / ROPE_DIM)
    ang = pos[:, None] * inv_freq[None, :]
    cos = jnp.cos(ang)[None, :, None, :]
    sin = jnp.sin(ang)[None, :, None, :]
    xf = x.astype(jnp.float32)
    x1 = xf[..., :half]
    x2 = xf[..., half:ROPE_DIM]
    y = jnp.concatenate([x1 * cos - x2 * sin, x2 * cos + x1 * sin, xf[..., ROPE_DIM:]], axis=-1)
    return y.astype(x.dtype)


def _sink_attention(qb, kb, vb, mask, sink):
    s = jnp.einsum('bnqkgd,bnskd->bnkgqs', qb, kb, preferred_element_type=jnp.float32) * (HEAD_DIM ** -0.5)
    s = jnp.where(mask[None, :, None, None], s, -jnp.inf)
    sk = sink.astype(jnp.float32)[None, None, :, :, None, None]
    m = jnp.maximum(jnp.max(s, axis=-1, keepdims=True), sk)
    e = jnp.exp(s - m)
    p = e / (jnp.sum(e, axis=-1, keepdims=True) + jnp.exp(sk - m))
    o = jnp.einsum('bnkgqs,bnskd->bnqkgd', p, vb.astype(jnp.float32))
    return o.astype(qb.dtype)


def _attn_prompt(q, k, v, sink):
    B, S = q.shape[0], q.shape[1]
    nc = S // CHUNK
    pad = WIN_CHUNKS * CHUNK
    kp = jnp.pad(k, ((0, 0), (pad, 0), (0, 0), (0, 0))).reshape(B, nc + WIN_CHUNKS, CHUNK, N_KV_HEADS, HEAD_DIM)
    vp = jnp.pad(v, ((0, 0), (pad, 0), (0, 0), (0, 0))).reshape(B, nc + WIN_CHUNKS, CHUNK, N_KV_HEADS, HEAD_DIM)
    kb = jnp.concatenate([kp[:, j:j + nc] for j in range(WIN_CHUNKS + 1)], axis=2)
    vb = jnp.concatenate([vp[:, j:j + nc] for j in range(WIN_CHUNKS + 1)], axis=2)
    n_keys = (WIN_CHUNKS + 1) * CHUNK
    key_chunk = jnp.arange(nc)[:, None] - WIN_CHUNKS + (jnp.arange(n_keys) // CHUNK)[None, :]
    mask = jnp.broadcast_to((key_chunk >= 0)[:, None, :], (nc, CHUNK, n_keys))
    qb = q.reshape(B, nc, CHUNK, N_KV_HEADS, GROUP, HEAD_DIM)
    o = _sink_attention(qb, kb, vb, mask, sink)
    return o.reshape(B, S, ATTN_DIM)


def _attn_sample(q, k, v, cache_k, cache_v, sink):
    B, T = q.shape[0], q.shape[1]
    kf = jnp.concatenate([cache_k.astype(k.dtype), k], axis=1)
    vf = jnp.concatenate([cache_v.astype(v.dtype), v], axis=1)
    qb = q.reshape(B, 1, T, N_KV_HEADS, GROUP, HEAD_DIM)
    mask = jnp.ones((1, T, WINDOW + T), dtype=bool)
    o = _sink_attention(qb, kf[:, None], vf[:, None], mask, sink)
    return o.reshape(B, T, ATTN_DIM), kf[:, -WINDOW:], vf[:, -WINDOW:]


def _depthwise_causal_conv(u, hist, w_dw, b_dw):
    up = jnp.concatenate([hist.astype(u.dtype), u], axis=1)
    y = lax.conv_general_dilated(up, w_dw[:, None, :].astype(u.dtype), window_strides=(1,), padding='VALID',
                                 dimension_numbers=('NWC', 'WIO', 'NWC'), feature_group_count=CONV_DIM)
    return y + b_dw.astype(u.dtype), up[:, -(CONV_WIDTH - 1):]


def _layer(x, pos, conv_hist, cache_k, cache_v, norm_mix_g, w_in, w_dw, b_dw, conv_ln_g, conv_ln_b,
           w_conv_out, q_norm_g, k_norm_g, sinks, w_attn_out, w_out, norm_ffn_g, w_gate_up, w_down):
    B, T = x.shape[0], x.shape[1]
    h = _rms_norm(x, norm_mix_g)
    z = h @ w_in
    a_lin, a_gate, q, k, v, g_conv, g_attn = jnp.split(z, np.cumsum(SPLIT_SIZES)[:-1].tolist(), axis=-1)
    u = a_lin * jax.nn.sigmoid(a_gate)
    if conv_hist is None:
        conv_hist = jnp.zeros((B, CONV_WIDTH - 1, CONV_DIM), u.dtype)
    c, new_hist = _depthwise_causal_conv(u, conv_hist, w_dw, b_dw)
    c = jax.nn.silu(_layer_norm(c, conv_ln_g, conv_ln_b))
    conv_out = c @ w_conv_out
    q = _partial_rope(_rms_norm(q.reshape(B, T, N_Q_HEADS, HEAD_DIM), q_norm_g), pos)
    k = _partial_rope(_rms_norm(k.reshape(B, T, N_KV_HEADS, HEAD_DIM), k_norm_g), pos)
    v = v.reshape(B, T, N_KV_HEADS, HEAD_DIM)
    sink = sinks.reshape(N_KV_HEADS, GROUP)
    if cache_k is None:
        o = _attn_prompt(q, k, v, sink)
        new_k = k[:, -WINDOW:]
        new_v = v[:, -WINDOW:]
    else:
        o, new_k, new_v = _attn_sample(q, k, v, cache_k, cache_v, sink)
    attn_out = o @ w_attn_out
    mix = jax.nn.sigmoid(g_conv) * conv_out + jax.nn.sigmoid(g_attn) * attn_out
    x = x + mix @ w_out
    hf = _rms_norm(x, norm_ffn_g)
    gate, up = jnp.split(hf @ w_gate_up, 2, axis=-1)
    x = x + (jax.nn.silu(gate) * up) @ w_down
    return x, new_hist, new_k, new_v


def _trunk(x, pos, state_conv, cache_k, cache_v, weights):
    hists, ks, vs = [], [], []
    for l in range(DEPTH):
        lw = [w[l] for w in weights]
        x, nh, nk, nv = _layer(x, pos,
                               None if state_conv is None else state_conv[l],
                               None if cache_k is None else cache_k[l],
                               None if cache_v is None else cache_v[l], *lw)
        hists.append(nh)
        ks.append(nk)
        vs.append(nv)
    return x, jnp.stack(hists), jnp.stack(ks), jnp.stack(vs)


def setup_inputs(seed: int = 0) -> dict:
    key = jax.random.key(seed)
    ks = jax.random.split(key, 20)
    f32 = jnp.float32
    nrm = lambda k, shape, scale: jax.random.normal(k, shape, f32) * scale
    return {
        'x_prompt': nrm(ks[0], (BATCH, SEQ, D_MODEL), 1.0),
        'x_sample': nrm(ks[1], (DEC_BATCH, DEC_SEQ, D_MODEL), 1.0),
        'state_conv': nrm(ks[2], (DEPTH, DEC_BATCH, CONV_WIDTH - 1, CONV_DIM), 0.5),
        'cache_k': nrm(ks[3], (DEPTH, DEC_BATCH, WINDOW, N_KV_HEADS, HEAD_DIM), 1.0),
        'cache_v': nrm(ks[4], (DEPTH, DEC_BATCH, WINDOW, N_KV_HEADS, HEAD_DIM), 1.0),
        'norm_mix_g': 1.0 + nrm(ks[5], (DEPTH, D_MODEL), 0.05),
        'w_in': nrm(ks[6], (DEPTH, D_MODEL, IN_DIM), D_MODEL ** -0.5),
        'w_dw': nrm(ks[7], (DEPTH, CONV_WIDTH, CONV_DIM), CONV_WIDTH ** -0.5),
        'b_dw': nrm(ks[8], (DEPTH, CONV_DIM), 0.02),
        'conv_ln_g': 1.0 + nrm(ks[9], (DEPTH, CONV_DIM), 0.05),
        'conv_ln_b': nrm(ks[10], (DEPTH, CONV_DIM), 0.02),
        'w_conv_out': nrm(ks[11], (DEPTH, CONV_DIM, D_MODEL), CONV_DIM ** -0.5),
        'q_norm_g': 1.0 + nrm(ks[12], (DEPTH, HEAD_DIM), 0.05),
        'k_norm_g': 1.0 + nrm(ks[13], (DEPTH, HEAD_DIM), 0.05),
        'sinks': nrm(ks[14], (DEPTH, N_Q_HEADS), 0.5),
        'w_attn_out': nrm(ks[15], (DEPTH, ATTN_DIM, D_MODEL), ATTN_DIM ** -0.5),
        'w_out': nrm(ks[16], (DEPTH, D_MODEL, D_MODEL), D_MODEL ** -0.5),
        'norm_ffn_g': 1.0 + nrm(ks[17], (DEPTH, D_MODEL), 0.05),
        'w_gate_up': nrm(ks[18], (DEPTH, D_MODEL, 2 * D_FF), D_MODEL ** -0.5),
        'w_down': nrm(ks[19], (DEPTH, D_FF, D_MODEL), D_FF ** -0.5),
    }


def reference(x_prompt, x_sample, state_conv, cache_k, cache_v, norm_mix_g, w_in, w_dw, b_dw, conv_ln_g,
              conv_ln_b, w_conv_out, q_norm_g, k_norm_g, sinks, w_attn_out, w_out, norm_ffn_g, w_gate_up, w_down):
    weights = (norm_mix_g, w_in, w_dw, b_dw, conv_ln_g, conv_ln_b, w_conv_out, q_norm_g, k_norm_g, sinks,
               w_attn_out, w_out, norm_ffn_g, w_gate_up, w_down)
    pos_p = jnp.arange(x_prompt.shape[1], dtype=jnp.float32)
    pos_s = jnp.arange(x_sample.shape[1], dtype=jnp.float32) + PAST_LEN
    y_prompt, conv_p, k_p, v_p = _trunk(x_prompt, pos_p, None, None, None, weights)
    y_sample, conv_s, k_s, v_s = _trunk(x_sample, pos_s, state_conv, cache_k, cache_v, weights)
    return (y_prompt, y_sample, conv_p, k_p, v_p, conv_s, k_s, v_s)
```

```python
import functools

import jax
import jax.numpy as jnp
from jax import lax
from jax.experimental import pallas as pl
from jax.experimental.pallas import tpu as pltpu

F32 = jnp.float32
BF16 = jnp.bfloat16

D_MODEL = 2048
DEPTH = 4
CHUNK = 64
CONV_DIM = D_MODEL // 2
CONV_WIDTH = 31
HEAD_DIM = 64
N_Q_HEADS = (D_MODEL // 2) // HEAD_DIM
N_KV_HEADS = 4
GROUP = N_Q_HEADS // N_KV_HEADS
ATTN_DIM = N_Q_HEADS * HEAD_DIM
KV_DIM = N_KV_HEADS * HEAD_DIM
QK_DIM = ATTN_DIM + KV_DIM
WINDOW = 128
ROPE_DIM = HEAD_DIM // 4
ROPE_THETA = 500000.0
D_FF = ((8 * D_MODEL + 3 * 256 - 1) // (3 * 256)) * 256
EPS = 1e-6
PAST_LEN = 4096

LANES = 128
MIB = 1024 * 1024

_NT = (((1,), (1,)), ((), ()))


def _params(sem, vmem_mib):
    return pltpu.CompilerParams(dimension_semantics=sem, vmem_limit_bytes=vmem_mib * MIB)


def _row_tile(m):
    return min(m, 1024)


def _rms_to_bf16(x, g):
    ms = jnp.mean(x * x, axis=-1, keepdims=True)
    return (x * lax.rsqrt(ms + EPS) * g).astype(BF16)


def _glu_kernel(x_ref, g_ref, wl_ref, wg_ref, u_ref, hb_ref):
    @pl.when(pl.program_id(1) == 0)
    def _():
        hb_ref[...] = _rms_to_bf16(x_ref[...], g_ref[...])

    h = hb_ref[...]
    a = jnp.dot(h, wl_ref[...], preferred_element_type=F32)
    b = jnp.dot(h, wg_ref[...], preferred_element_type=F32)
    u_ref[...] = a * jax.nn.sigmoid(b)


def _glu(x, g, wl, wg, l):
    m = x.shape[0]
    tm, tn = _row_tile(m), 512
    return pl.pallas_call(
        _glu_kernel,
        grid=(m // tm, CONV_DIM // tn),
        in_specs=[
            pl.BlockSpec((tm, D_MODEL), lambda i, j: (i, 0)),
            pl.BlockSpec((None, 1, D_MODEL), lambda i, j: (l, 0, 0)),
            pl.BlockSpec((None, D_MODEL, tn), lambda i, j: (l, 0, j)),
            pl.BlockSpec((None, D_MODEL, tn), lambda i, j: (l, 0, j)),
        ],
        out_specs=[
            pl.BlockSpec((tm, tn), lambda i, j: (i, j)),
            pl.BlockSpec((tm, D_MODEL), lambda i, j: (i, 0)),
        ],
        out_shape=[
            jax.ShapeDtypeStruct((m, CONV_DIM), F32),
            jax.ShapeDtypeStruct((m, D_MODEL), BF16),
        ],
        compiler_params=_params(("arbitrary", "arbitrary"), 48),
        name="glu",
    )(x, g, wl, wg)


def _split_bf16(a):
    hi = a.astype(BF16)
    lo = (a - hi.astype(F32)).astype(BF16)
    return hi, lo


def _qkv_kernel(hb_ref, w_ref, cos_ref, sa_ref, sb_ref, nw_ref, e_ref, et_ref, q_ref, k_ref, v_ref):
    z = jnp.dot(hb_ref[...], w_ref[...], preferred_element_type=F32)
    v_ref[...] = z[:, QK_DIM:]
    qk = z[:, :QK_DIM]
    hi, lo = _split_bf16(qk * qk)
    e = e_ref[...]
    ms = jnp.dot(hi, e, preferred_element_type=F32) + jnp.dot(lo, e, preferred_element_type=F32)
    rhi, rlo = _split_bf16(lax.rsqrt(ms + EPS))
    et = et_ref[...]
    scale = jnp.dot(rhi, et, preferred_element_type=F32) + jnp.dot(rlo, et, preferred_element_type=F32)
    y = qk * scale * nw_ref[...]
    cos, sa, sb = cos_ref[...], sa_ref[...], sb_ref[...]
    for cg in range(QK_DIM // LANES):
        yc = y[:, cg * LANES:(cg + 1) * LANES]
        rot = yc * cos + pltpu.roll(yc, LANES - ROPE_DIM // 2, 1) * sa + pltpu.roll(yc, ROPE_DIM // 2, 1) * sb
        if cg < ATTN_DIM // LANES:
            q_ref[:, cg * LANES:(cg + 1) * LANES] = (rot * (HEAD_DIM ** -0.5)).astype(BF16)
        else:
            c0 = cg * LANES - ATTN_DIM
            k_ref[:, c0:c0 + LANES] = rot


def _qkv(hb, w, tabs, nw, e, et, l):
    m = hb.shape[0]
    tm = min(m, 512)
    cos, sa, sb = tabs
    tab_spec = pl.BlockSpec((tm, LANES), lambda i: (i, 0))
    return pl.pallas_call(
        _qkv_kernel,
        grid=(m // tm,),
        in_specs=[
            pl.BlockSpec((tm, D_MODEL), lambda i: (i, 0)),
            pl.BlockSpec((None, D_MODEL, QK_DIM + KV_DIM), lambda i: (l, 0, 0)),
            tab_spec, tab_spec, tab_spec,
            pl.BlockSpec((None, 1, QK_DIM), lambda i: (l, 0, 0)),
            pl.BlockSpec((QK_DIM, LANES), lambda i: (0, 0)),
            pl.BlockSpec((LANES, QK_DIM), lambda i: (0, 0)),
        ],
        out_specs=[
            pl.BlockSpec((tm, ATTN_DIM), lambda i: (i, 0)),
            pl.BlockSpec((tm, KV_DIM), lambda i: (i, 0)),
            pl.BlockSpec((tm, KV_DIM), lambda i: (i, 0)),
        ],
        out_shape=[
            jax.ShapeDtypeStruct((m, ATTN_DIM), BF16),
            jax.ShapeDtypeStruct((m, KV_DIM), F32),
            jax.ShapeDtypeStruct((m, KV_DIM), F32),
        ],
        compiler_params=_params(("arbitrary",), 56),
        name="qkv",
    )(hb, w, cos, sa, sb, nw, e, et)


CONV_ROWS = 32
HALO = 32


def _ln_swish_bf16(acc, lg, lb):
    mean = jnp.mean(acc, axis=-1, keepdims=True)
    xc = acc - mean
    var = jnp.mean(xc * xc, axis=-1, keepdims=True)
    y = xc * lax.rsqrt(var + EPS) * lg + lb
    return (y * jax.nn.sigmoid(y)).astype(BF16)


def _conv_kernel(u_ref, halo_ref, w_ref, b_ref, lg_ref, lb_ref, c_ref, up_ref, *, tr):
    i = pl.program_id(0)

    @pl.when(i == 0)
    def _():
        up_ref[0:HALO, :] = jnp.zeros((HALO, CONV_DIM), F32)

    @pl.when(i > 0)
    def _():
        up_ref[0:HALO, :] = halo_ref[...]

    up_ref[HALO:, :] = u_ref[...]
    first = HALO - (CONV_WIDTH - 1)
    for r in range(tr // CONV_ROWS):
        r0 = r * CONV_ROWS
        acc = jnp.broadcast_to(b_ref[...], (CONV_ROWS, CONV_DIM))
        for j in range(CONV_WIDTH):
            s = r0 + first + j
            acc = acc + w_ref[j:j + 1, :] * up_ref[s:s + CONV_ROWS, :]
        c_ref[r0:r0 + CONV_ROWS, :] = _ln_swish_bf16(acc, lg_ref[...], lb_ref[...])


def _conv(u, w, b, lg, lb, l):
    m = u.shape[0]
    tr = 256
    per = tr // HALO
    vec = pl.BlockSpec((None, 1, CONV_DIM), lambda i: (l, 0, 0))
    return pl.pallas_call(
        functools.partial(_conv_kernel, tr=tr),
        grid=(m // tr,),
        in_specs=[
            pl.BlockSpec((tr, CONV_DIM), lambda i: (i, 0)),
            pl.BlockSpec((HALO, CONV_DIM), lambda i: (jnp.maximum(i * per - 1, 0), 0)),
            pl.BlockSpec((None, CONV_WIDTH, CONV_DIM), lambda i: (l, 0, 0)),
            vec, vec, vec,
        ],
        out_specs=pl.BlockSpec((tr, CONV_DIM), lambda i: (i, 0)),
        out_shape=jax.ShapeDtypeStruct((m, CONV_DIM), BF16),
        scratch_shapes=[pltpu.VMEM((HALO + tr, CONV_DIM), F32)],
        compiler_params=_params(("arbitrary",), 48),
        name="conv",
    )(u, u, w, b, lg, lb)


CONV_S_BATCH = 8


def _conv_s_kernel(u_ref, hist_ref, w_ref, b_ref, lg_ref, lb_ref, c_ref, nh_ref, up_ref, *, t):
    nhist = CONV_WIDTH - 1
    for bb in range(CONV_S_BATCH):
        up_ref[0:nhist, :] = hist_ref[bb]
        up_ref[nhist:nhist + t, :] = u_ref[bb]
        acc = jnp.broadcast_to(b_ref[...], (t, CONV_DIM))
        for j in range(CONV_WIDTH):
            acc = acc + w_ref[j:j + 1, :] * up_ref[j:j + t, :]
        c_ref[bb] = _ln_swish_bf16(acc, lg_ref[...], lb_ref[...])
        nh_ref[bb] = up_ref[t:t + nhist, :]


def _conv_s(u, hist, w, b, lg, lb, l):
    nb, t, _ = u.shape
    nhist = CONV_WIDTH - 1
    vec = pl.BlockSpec((None, 1, CONV_DIM), lambda i: (l, 0, 0))
    return pl.pallas_call(
        functools.partial(_conv_s_kernel, t=t),
        grid=(nb // CONV_S_BATCH,),
        in_specs=[
            pl.BlockSpec((CONV_S_BATCH, t, CONV_DIM), lambda i: (i, 0, 0)),
            pl.BlockSpec((None, CONV_S_BATCH, nhist, CONV_DIM), lambda i: (l, i, 0, 0)),
            pl.BlockSpec((None, CONV_WIDTH, CONV_DIM), lambda i: (l, 0, 0)),
            vec, vec, vec,
        ],
        out_specs=[
            pl.BlockSpec((CONV_S_BATCH, t, CONV_DIM), lambda i: (i, 0, 0)),
            pl.BlockSpec((CONV_S_BATCH, nhist, CONV_DIM), lambda i: (i, 0, 0)),
        ],
        out_shape=[
            jax.ShapeDtypeStruct((nb, t, CONV_DIM), BF16),
            jax.ShapeDtypeStruct((nb, nhist, CONV_DIM), F32),
        ],
        scratch_shapes=[pltpu.VMEM((nhist + t + 2, CONV_DIM), F32)],
        compiler_params=_params(("arbitrary",), 32),
        name="conv_s",
    )(u, hist, w, b, lg, lb)


ATT_Q = 2 * CHUNK
ATT_K = 4 * CHUNK
ATT_BLOCK = 512


def _sink_column(sink_ref, kh, rows_per_head):
    n = GROUP * rows_per_head
    row = lax.broadcasted_iota(jnp.int32, (n, 1), 0)
    col = jnp.full((n, 1), sink_ref[kh * GROUP + GROUP - 1], F32)
    for g in range(GROUP - 2, -1, -1):
        col = jnp.where(row < (g + 1) * rows_per_head, sink_ref[kh * GROUP + g], col)
    return col


def _stack_heads(q_ref, r0, rows, kh):
    parts = [q_ref[r0:r0 + rows, (kh * GROUP + g) * HEAD_DIM:(kh * GROUP + g + 1) * HEAD_DIM] for g in range(GROUP)]
    return jnp.concatenate(parts, axis=0)


def _attn_kernel(sink_ref, q_ref, k_ref, kh_ref, v_ref, vh_ref, o_ref, kbuf, vbuf):
    i = pl.program_id(0)
    kbuf[0:WINDOW, :] = kh_ref[...].astype(BF16)
    kbuf[WINDOW:, :] = k_ref[...].astype(BF16)
    vbuf[0:WINDOW, :] = vh_ref[...].astype(BF16)
    vbuf[WINDOW:, :] = v_ref[...].astype(BF16)

    n = GROUP * ATT_Q
    row_chunk = (lax.broadcasted_iota(jnp.int32, (n, ATT_K), 0) % ATT_Q) // CHUNK
    key_chunk = lax.broadcasted_iota(jnp.int32, (n, ATT_K), 1) // CHUNK
    band = (key_chunk >= row_chunk) & (key_chunk <= row_chunk + 2)
    band_first = band & ((key_chunk >= 2) | (i > 0))

    for t in range(ATT_BLOCK // ATT_Q):
        r0 = t * ATT_Q
        mask = band_first if t == 0 else band
        for kh in range(N_KV_HEADS):
            qs = _stack_heads(q_ref, r0, ATT_Q, kh)
            kk = kbuf[r0:r0 + ATT_K, kh * HEAD_DIM:(kh + 1) * HEAD_DIM]
            vv = vbuf[r0:r0 + ATT_K, kh * HEAD_DIM:(kh + 1) * HEAD_DIM]
            s = lax.dot_general(qs, kk, _NT, preferred_element_type=F32)
            s = jnp.where(mask, s, -jnp.inf)
            sink = _sink_column(sink_ref, kh, ATT_Q)
            mx = jnp.maximum(jnp.max(s, axis=-1, keepdims=True), sink)
            e = jnp.exp(s - mx)
            den = jnp.sum(e, axis=-1, keepdims=True) + jnp.exp(sink - mx)
            o = jnp.dot(e.astype(BF16), vv, preferred_element_type=F32) / den
            o = jnp.concatenate([o[g * ATT_Q:(g + 1) * ATT_Q] for g in range(GROUP)], axis=1)
            o_ref[r0:r0 + ATT_Q, kh * GROUP * HEAD_DIM:(kh + 1) * GROUP * HEAD_DIM] = o.astype(BF16)


def _attn(q, k, v, sinks):
    m = q.shape[0]
    per = ATT_BLOCK // WINDOW
    main = pl.BlockSpec((ATT_BLOCK, KV_DIM), lambda i: (i, 0))
    halo = pl.BlockSpec((WINDOW, KV_DIM), lambda i: (jnp.maximum(i * per - 1, 0), 0))
    return pl.pallas_call(
        _attn_kernel,
        grid=(m // ATT_BLOCK,),
        in_specs=[
            pl.BlockSpec(memory_space=pltpu.SMEM),
            pl.BlockSpec((ATT_BLOCK, ATTN_DIM), lambda i: (i, 0)),
            main, halo, main, halo,
        ],
        out_specs=pl.BlockSpec((ATT_BLOCK, ATTN_DIM), lambda i: (i, 0)),
        out_shape=jax.ShapeDtypeStruct((m, ATTN_DIM), BF16),
        scratch_shapes=[pltpu.VMEM((WINDOW + ATT_BLOCK, KV_DIM), BF16)] * 2,
        compiler_params=_params(("arbitrary",), 32),
        name="attn",
    )(sinks, q, k, k, v, v)


def _attn_s_kernel(sink_ref, q_ref, k_ref, v_ref, ck_ref, cv_ref, o_ref, nk_ref, nv_ref, *, t):
    keep = WINDOW - t
    nk_ref[0:keep, :] = ck_ref[t:, :]
    nk_ref[keep:, :] = k_ref[...]
    nv_ref[0:keep, :] = cv_ref[t:, :]
    nv_ref[keep:, :] = v_ref[...]
    for kh in range(N_KV_HEADS):
        cols = slice(kh * HEAD_DIM, (kh + 1) * HEAD_DIM)
        qs = _stack_heads(q_ref, 0, t, kh)
        kc = ck_ref[:, cols].astype(BF16)
        kn = k_ref[:, cols].astype(BF16)
        s1 = lax.dot_general(qs, kc, _NT, preferred_element_type=F32)
        s2 = lax.dot_general(qs, kn, _NT, preferred_element_type=F32)
        sink = _sink_column(sink_ref, kh, t)
        mx = jnp.maximum(jnp.maximum(jnp.max(s1, axis=-1, keepdims=True), jnp.max(s2, axis=-1, keepdims=True)), sink)
        e1 = jnp.exp(s1 - mx)
        e2 = jnp.exp(s2 - mx)
        den = jnp.sum(e1, axis=-1, keepdims=True) + jnp.sum(e2, axis=-1, keepdims=True) + jnp.exp(sink - mx)
        o = jnp.dot(e1.astype(BF16), cv_ref[:, cols].astype(BF16), preferred_element_type=F32)
        o = o + jnp.dot(e2.astype(BF16), v_ref[:, cols].astype(BF16), preferred_element_type=F32)
        o = o / den
        o = jnp.concatenate([o[g * t:(g + 1) * t] for g in range(GROUP)], axis=1)
        o_ref[:, kh * GROUP * HEAD_DIM:(kh + 1) * GROUP * HEAD_DIM] = o.astype(BF16)


def _attn_s(q, k, v, cache_k, cache_v, sinks, l, t):
    nb = q.shape[0] // t
    new = pl.BlockSpec((t, KV_DIM), lambda b: (b, 0))
    cache = pl.BlockSpec((None, None, WINDOW, KV_DIM), lambda b: (l, b, 0, 0))
    out_cache = pl.BlockSpec((None, WINDOW, KV_DIM), lambda b: (b, 0, 0))
    return pl.pallas_call(
        functools.partial(_attn_s_kernel, t=t),
        grid=(nb,),
        in_specs=[
            pl.BlockSpec(memory_space=pltpu.SMEM),
            pl.BlockSpec((t, ATTN_DIM), lambda b: (b, 0)),
            new, new, cache, cache,
        ],
        out_specs=[pl.BlockSpec((t, ATTN_DIM), lambda b: (b, 0)), out_cache, out_cache],
        out_shape=[
            jax.ShapeDtypeStruct((nb * t, ATTN_DIM), BF16),
            jax.ShapeDtypeStruct((nb, WINDOW, KV_DIM), F32),
            jax.ShapeDtypeStruct((nb, WINDOW, KV_DIM), F32),
        ],
        compiler_params=_params(("arbitrary",), 32),
        name="attn_s",
    )(sinks, q, k, v, cache_k, cache_v)


def _merge_kernel(hb_ref, c_ref, o_ref, wgc_ref, wga_ref, wco_ref, wao_ref, mix_ref):
    h = hb_ref[...]
    gc = jax.nn.sigmoid(jnp.dot(h, wgc_ref[...], preferred_element_type=F32))
    conv_out = jnp.dot(c_ref[...], wco_ref[...], preferred_element_type=F32)
    ga = jax.nn.sigmoid(jnp.dot(h, wga_ref[...], preferred_element_type=F32))
    attn_out = jnp.dot(o_ref[...], wao_ref[...], preferred_element_type=F32)
    mix_ref[...] = (gc * conv_out + ga * attn_out).astype(BF16)


def _merge(hb, c, o, wgc, wga, wco, wao, l):
    m = hb.shape[0]
    tm, tn = _row_tile(m), 256
    wide = pl.BlockSpec((None, D_MODEL, tn), lambda i, j: (l, 0, j))
    narrow = pl.BlockSpec((None, CONV_DIM, tn), lambda i, j: (l, 0, j))
    rows = pl.BlockSpec((tm, CONV_DIM), lambda i, j: (i, 0))
    return pl.pallas_call(
        _merge_kernel,
        grid=(m // tm, D_MODEL // tn),
        in_specs=[pl.BlockSpec((tm, D_MODEL), lambda i, j: (i, 0)), rows, rows, wide, wide, narrow, narrow],
        out_specs=pl.BlockSpec((tm, tn), lambda i, j: (i, j)),
        out_shape=jax.ShapeDtypeStruct((m, D_MODEL), BF16),
        compiler_params=_params(("arbitrary", "arbitrary"), 48),
        name="merge",
    )(hb, c, o, wgc, wga, wco, wao)


def _outproj_kernel(x_ref, mix_ref, w_ref, y_ref):
    y_ref[...] = x_ref[...] + jnp.dot(mix_ref[...], w_ref[...], preferred_element_type=F32)


def _outproj(x, mix, w, l):
    m = x.shape[0]
    tm, tn = _row_tile(m), 512
    return pl.pallas_call(
        _outproj_kernel,
        grid=(m // tm, D_MODEL // tn),
        in_specs=[
            pl.BlockSpec((tm, tn), lambda i, j: (i, j)),
            pl.BlockSpec((tm, D_MODEL), lambda i, j: (i, 0)),
            pl.BlockSpec((None, D_MODEL, tn), lambda i, j: (l, 0, j)),
        ],
        out_specs=pl.BlockSpec((tm, tn), lambda i, j: (i, j)),
        out_shape=jax.ShapeDtypeStruct((m, D_MODEL), F32),
        compiler_params=_params(("arbitrary", "arbitrary"), 48),
        name="outproj",
    )(x, mix, w)


def _ffn_kernel(x_ref, g_ref, wg_ref, wu_ref, wd_ref, y_ref, hb_ref):
    @pl.when(pl.program_id(1) == 0)
    def _():
        x = x_ref[...]
        hb_ref[...] = _rms_to_bf16(x, g_ref[...])
        y_ref[...] = x

    h = hb_ref[...]
    gate = jnp.dot(h, wg_ref[...], preferred_element_type=F32)
    up = jnp.dot(h, wu_ref[...], preferred_element_type=F32)
    act = (gate * jax.nn.sigmoid(gate) * up).astype(BF16)
    y_ref[...] += jnp.dot(act, wd_ref[...], preferred_element_type=F32)


def _ffn(x, g, wg, wu, wd, l):
    m = x.shape[0]
    tm, tf = min(m, 512), 512
    return pl.pallas_call(
        _ffn_kernel,
        grid=(m // tm, D_FF // tf),
        in_specs=[
            pl.BlockSpec((tm, D_MODEL), lambda i, j: (i, 0)),
            pl.BlockSpec((None, 1, D_MODEL), lambda i, j: (l, 0, 0)),
            pl.BlockSpec((None, D_MODEL, tf), lambda i, j: (l, 0, j)),
            pl.BlockSpec((None, D_MODEL, tf), lambda i, j: (l, 0, j)),
            pl.BlockSpec((None, tf, D_MODEL), lambda i, j: (l, j, 0)),
        ],
        out_specs=pl.BlockSpec((tm, D_MODEL), lambda i, j: (i, 0)),
        out_shape=jax.ShapeDtypeStruct((m, D_MODEL), F32),
        scratch_shapes=[pltpu.VMEM((tm, D_MODEL), BF16)],
        compiler_params=_params(("arbitrary", "arbitrary"), 48),
        name="ffn",
    )(x, g, wg, wu, wd)


def _rope_tables(pos):
    half = ROPE_DIM // 2
    inv_freq = ROPE_THETA ** (-jnp.arange(half, dtype=F32) * 2.0 / ROPE_DIM)
    ang = pos[:, None] * inv_freq[None, :]
    cos, sin = jnp.cos(ang), jnp.sin(ang)
    n = pos.shape[0]
    rest = HEAD_DIM - ROPE_DIM
    zeros_h = jnp.zeros((n, half), F32)
    cos_t = jnp.concatenate([cos, cos, jnp.ones((n, rest), F32)], axis=1)
    sa_t = jnp.concatenate([-sin, zeros_h, jnp.zeros((n, rest), F32)], axis=1)
    sb_t = jnp.concatenate([zeros_h, sin, jnp.zeros((n, rest), F32)], axis=1)
    reps = LANES // HEAD_DIM
    return tuple(jnp.tile(a, (1, reps)) for a in (cos_t, sa_t, sb_t))


def _head_pool_matrices():
    head = jnp.arange(QK_DIM, dtype=jnp.int32) // HEAD_DIM
    onehot = head[:, None] == jnp.arange(LANES, dtype=jnp.int32)[None, :]
    e = jnp.where(onehot, 1.0 / HEAD_DIM, 0.0).astype(BF16)
    et = jnp.where(onehot.T, 1.0, 0.0).astype(BF16)
    return e, et


def kernel(x_prompt, x_sample, state_conv, cache_k, cache_v, norm_mix_g, w_in, w_dw, b_dw, conv_ln_g,
           conv_ln_b, w_conv_out, q_norm_g, k_norm_g, sinks, w_attn_out, w_out, norm_ffn_g, w_gate_up, w_down):
    nb, t = x_sample.shape[0], x_sample.shape[1]
    seq = x_prompt.shape[1]

    def cut(w, a, b):
        return w[:, :, a:b].astype(BF16)

    c1 = CONV_DIM
    c2 = c1 + CONV_DIM
    c3 = c2 + QK_DIM + KV_DIM
    c4 = c3 + D_MODEL
    w_lin, w_gate, w_qkv = cut(w_in, 0, c1), cut(w_in, c1, c2), cut(w_in, c2, c3)
    w_gc, w_ga = cut(w_in, c3, c4), cut(w_in, c4, c4 + D_MODEL)
    w_ffg, w_ffu = cut(w_gate_up, 0, D_FF), cut(w_gate_up, D_FF, 2 * D_FF)
    w_co, w_ao, w_o, w_dn = (w.astype(BF16) for w in (w_conv_out, w_attn_out, w_out, w_down))

    g_mix = norm_mix_g.reshape(DEPTH, 1, D_MODEL)
    g_ffn = norm_ffn_g.reshape(DEPTH, 1, D_MODEL)
    b_conv = b_dw.reshape(DEPTH, 1, CONV_DIM)
    ln_g = conv_ln_g.reshape(DEPTH, 1, CONV_DIM)
    ln_b = conv_ln_b.reshape(DEPTH, 1, CONV_DIM)
    qk_w = jnp.concatenate([jnp.tile(q_norm_g, (1, N_Q_HEADS)), jnp.tile(k_norm_g, (1, N_KV_HEADS))], axis=1)
    qk_w = qk_w.reshape(DEPTH, 1, QK_DIM)
    e, et = _head_pool_matrices()

    tabs_p = _rope_tables(jnp.arange(seq, dtype=F32))
    tabs_s = _rope_tables(jnp.tile(jnp.arange(t, dtype=F32) + PAST_LEN, nb))
    ck = cache_k.reshape(DEPTH, nb, WINDOW, KV_DIM)
    cv = cache_v.reshape(DEPTH, nb, WINDOW, KV_DIM)

    def token_tail(x, hb, c, o, l):
        mix = _merge(hb, c, o, w_gc, w_ga, w_co, w_ao, l)
        x = _outproj(x, mix, w_o, l)
        return _ffn(x, g_ffn, w_ffg, w_ffu, w_dn, l)

    xp = x_prompt.reshape(seq, D_MODEL)
    xs = x_sample.reshape(nb * t, D_MODEL)
    hist_p, k_p, v_p, hist_s, k_s, v_s = [], [], [], [], [], []
    for l in range(DEPTH):
        u, hb = _glu(xp, g_mix, w_lin, w_gate, l)
        q, k, v = _qkv(hb, w_qkv, tabs_p, qk_w, e, et, l)
        c = _conv(u, w_dw, b_conv, ln_g, ln_b, l)
        o = _attn(q, k, v, sinks[l])
        xp = token_tail(xp, hb, c, o, l)
        hist_p.append(u[seq - (CONV_WIDTH - 1):][None])
        k_p.append(k[seq - WINDOW:].reshape(1, WINDOW, N_KV_HEADS, HEAD_DIM))
        v_p.append(v[seq - WINDOW:].reshape(1, WINDOW, N_KV_HEADS, HEAD_DIM))

        u, hb = _glu(xs, g_mix, w_lin, w_gate, l)
        q, k, v = _qkv(hb, w_qkv, tabs_s, qk_w, e, et, l)
        c, nh = _conv_s(u.reshape(nb, t, CONV_DIM), state_conv, w_dw, b_conv, ln_g, ln_b, l)
        o, nk, nv = _attn_s(q, k, v, ck, cv, sinks[l], l, t)
        xs = token_tail(xs, hb, c.reshape(nb * t, CONV_DIM), o, l)
        hist_s.append(nh)
        k_s.append(nk.reshape(nb, WINDOW, N_KV_HEADS, HEAD_DIM))
        v_s.append(nv.reshape(nb, WINDOW, N_KV_HEADS, HEAD_DIM))

    return (xp.reshape(x_prompt.shape), xs.reshape(x_sample.shape),
            jnp.stack(hist_p), jnp.stack(k_p), jnp.stack(v_p),
            jnp.stack(hist_s), jnp.stack(k_s), jnp.stack(v_s))
```

```python
import functools

import jax
import jax.numpy as jnp
from jax import lax
from jax.experimental import pallas as pl
from jax.experimental.pallas import tpu as pltpu

F32 = jnp.float32
BF16 = jnp.bfloat16

D_MODEL = 2048
DEPTH = 4
CHUNK = 64
CONV_DIM = D_MODEL // 2
CONV_WIDTH = 31
HEAD_DIM = 64
N_Q_HEADS = (D_MODEL // 2) // HEAD_DIM
N_KV_HEADS = 4
GROUP = N_Q_HEADS // N_KV_HEADS
ATTN_DIM = N_Q_HEADS * HEAD_DIM
KV_DIM = N_KV_HEADS * HEAD_DIM
QK_DIM = ATTN_DIM + KV_DIM
WINDOW = 128
ROPE_DIM = HEAD_DIM // 4
ROPE_THETA = 500000.0
D_FF = ((8 * D_MODEL + 3 * 256 - 1) // (3 * 256)) * 256
EPS = 1e-6
PAST_LEN = 4096

LANES = 128
SUBLANES = 8
MIB = 1024 * 1024
CONV_TILES = CONV_DIM // LANES

_NT = (((1,), (1,)), ((), ()))


def _params(sem, vmem_mib):
    return pltpu.CompilerParams(dimension_semantics=sem, vmem_limit_bytes=vmem_mib * MIB)


def _row_tile(m):
    return min(m, 1024)


def _rms_to_bf16(x, g):
    ms = jnp.mean(x * x, axis=-1, keepdims=True)
    return (x * lax.rsqrt(ms + EPS) * g).astype(BF16)


def _norm_kernel(x_ref, g_ref, hb_ref):
    hb_ref[...] = _rms_to_bf16(x_ref[...], g_ref[...])


def _norm(x, g, l):
    m = x.shape[0]
    tm = min(m, 512)
    return pl.pallas_call(
        _norm_kernel,
        grid=(m // tm,),
        in_specs=[
            pl.BlockSpec((tm, D_MODEL), lambda i: (i, 0)),
            pl.BlockSpec((None, 1, D_MODEL), lambda i: (l, 0, 0)),
        ],
        out_specs=pl.BlockSpec((tm, D_MODEL), lambda i: (i, 0)),
        out_shape=jax.ShapeDtypeStruct((m, D_MODEL), BF16),
        compiler_params=_params(("arbitrary",), 32),
        name="norm",
    )(x, g)


GLU_TN = 512


def _glu_kernel(hb_ref, wl_ref, wg_ref, u_ref):
    h = hb_ref[...]
    a = jnp.dot(h, wl_ref[...], preferred_element_type=F32)
    b = jnp.dot(h, wg_ref[...], preferred_element_type=F32)
    u = a * jax.nn.sigmoid(b)
    for t in range(GLU_TN // LANES):
        u_ref[t] = u[:, t * LANES:(t + 1) * LANES]


def _glu(hb, wl, wg, l):
    m = hb.shape[0]
    tm, tn = _row_tile(m), GLU_TN
    return pl.pallas_call(
        _glu_kernel,
        grid=(m // tm, CONV_DIM // tn),
        in_specs=[
            pl.BlockSpec((tm, D_MODEL), lambda i, j: (i, 0)),
            pl.BlockSpec((None, D_MODEL, tn), lambda i, j: (l, 0, j)),
            pl.BlockSpec((None, D_MODEL, tn), lambda i, j: (l, 0, j)),
        ],
        out_specs=pl.BlockSpec((tn // LANES, tm, LANES), lambda i, j: (j, i, 0)),
        out_shape=jax.ShapeDtypeStruct((CONV_TILES, m, LANES), F32),
        compiler_params=_params(("arbitrary", "arbitrary"), 48),
        name="glu",
    )(hb, wl, wg)


def _split_bf16(a):
    hi = a.astype(BF16)
    lo = (a - hi.astype(F32)).astype(BF16)
    return hi, lo


def _qkv_kernel(hb_ref, w_ref, cos_ref, sa_ref, sb_ref, nw_ref, e_ref, et_ref, q_ref, k_ref, v_ref):
    z = jnp.dot(hb_ref[...], w_ref[...], preferred_element_type=F32)
    v_ref[...] = z[:, QK_DIM:]
    qk = z[:, :QK_DIM]
    hi, lo = _split_bf16(qk * qk)
    e = e_ref[...]
    ms = jnp.dot(hi, e, preferred_element_type=F32) + jnp.dot(lo, e, preferred_element_type=F32)
    rhi, rlo = _split_bf16(lax.rsqrt(ms + EPS))
    et = et_ref[...]
    scale = jnp.dot(rhi, et, preferred_element_type=F32) + jnp.dot(rlo, et, preferred_element_type=F32)
    y = qk * scale * nw_ref[...]
    cos, sa, sb = cos_ref[...], sa_ref[...], sb_ref[...]
    for cg in range(QK_DIM // LANES):
        yc = y[:, cg * LANES:(cg + 1) * LANES]
        rot = yc * cos + pltpu.roll(yc, LANES - ROPE_DIM // 2, 1) * sa + pltpu.roll(yc, ROPE_DIM // 2, 1) * sb
        if cg < ATTN_DIM // LANES:
            q_ref[:, cg * LANES:(cg + 1) * LANES] = (rot * (HEAD_DIM ** -0.5)).astype(BF16)
        else:
            c0 = cg * LANES - ATTN_DIM
            k_ref[:, c0:c0 + LANES] = rot


def _qkv(hb, w, tabs, nw, e, et, l):
    m = hb.shape[0]
    tm = min(m, 512)
    cos, sa, sb = tabs
    tab_spec = pl.BlockSpec((tm, LANES), lambda i: (i, 0))
    return pl.pallas_call(
        _qkv_kernel,
        grid=(m // tm,),
        in_specs=[
            pl.BlockSpec((tm, D_MODEL), lambda i: (i, 0)),
            pl.BlockSpec((None, D_MODEL, QK_DIM + KV_DIM), lambda i: (l, 0, 0)),
            tab_spec, tab_spec, tab_spec,
            pl.BlockSpec((None, 1, QK_DIM), lambda i: (l, 0, 0)),
            pl.BlockSpec((QK_DIM, LANES), lambda i: (0, 0)),
            pl.BlockSpec((LANES, QK_DIM), lambda i: (0, 0)),
        ],
        out_specs=[
            pl.BlockSpec((tm, ATTN_DIM), lambda i: (i, 0)),
            pl.BlockSpec((tm, KV_DIM), lambda i: (i, 0)),
            pl.BlockSpec((tm, KV_DIM), lambda i: (i, 0)),
        ],
        out_shape=[
            jax.ShapeDtypeStruct((m, ATTN_DIM), BF16),
            jax.ShapeDtypeStruct((m, KV_DIM), F32),
            jax.ShapeDtypeStruct((m, KV_DIM), F32),
        ],
        compiler_params=_params(("arbitrary",), 56),
        name="qkv",
    )(hb, w, cos, sa, sb, nw, e, et)


CONV_BLOCK = 256
CONV_ROWS = 64
LN_ROWS = 32
HALO = 32


def _conv_kernel(u_ref, halo_ref, w_ref, b_ref, lg_ref, lb_ref, c_ref, up_ref, y_ref):
    i = pl.program_id(0)

    @pl.when(i == 0)
    def _():
        up_ref[:, 0:HALO, :] = jnp.zeros((CONV_TILES, HALO, LANES), F32)

    @pl.when(i > 0)
    def _():
        up_ref[:, 0:HALO, :] = halo_ref[...]

    up_ref[:, HALO:, :] = u_ref[...]
    first = HALO - (CONV_WIDTH - 1)
    groups = CONV_ROWS // SUBLANES
    for ct in range(CONV_TILES):
        taps = [w_ref[ct, pl.ds(j, SUBLANES, stride=0), :] for j in range(CONV_WIDTH)]
        bias = b_ref[ct, pl.ds(0, SUBLANES, stride=0), :]
        for rc in range(CONV_BLOCK // CONV_ROWS):
            acc = jnp.broadcast_to(bias, (groups, SUBLANES, LANES))
            for j in range(CONV_WIDTH):
                s = rc * CONV_ROWS + first + j
                acc = acc + taps[j] * up_ref[ct, s:s + CONV_ROWS, :].reshape(groups, SUBLANES, LANES)
            y_ref[ct, rc * CONV_ROWS:(rc + 1) * CONV_ROWS, :] = acc.reshape(CONV_ROWS, LANES)

    lg, lb = lg_ref[...], lb_ref[...]
    for r in range(CONV_BLOCK // LN_ROWS):
        r0 = r * LN_ROWS
        y = y_ref[:, r0:r0 + LN_ROWS, :]
        mean = jnp.sum(jnp.sum(y, axis=0, keepdims=True), axis=2, keepdims=True) * (1.0 / CONV_DIM)
        yc = y - mean
        var = jnp.sum(jnp.sum(yc * yc, axis=0, keepdims=True), axis=2, keepdims=True) * (1.0 / CONV_DIM)
        z = yc * lax.rsqrt(var + EPS) * lg + lb
        z = (z * jax.nn.sigmoid(z)).astype(BF16)
        for ct in range(CONV_TILES):
            c_ref[r0:r0 + LN_ROWS, ct * LANES:(ct + 1) * LANES] = z[ct]


def _conv(u3, w, b, lg, lb, l):
    m = u3.shape[1]
    tr = CONV_BLOCK
    per = tr // HALO
    tiled = lambda a: a.reshape(DEPTH, CONV_TILES, 1, LANES)
    w_tiles = w.reshape(DEPTH, CONV_WIDTH, CONV_TILES, LANES).transpose(0, 2, 1, 3)
    tvec = pl.BlockSpec((None, CONV_TILES, 1, LANES), lambda i: (l, 0, 0, 0))
    return pl.pallas_call(
        _conv_kernel,
        grid=(m // tr,),
        in_specs=[
            pl.BlockSpec((CONV_TILES, tr, LANES), lambda i: (0, i, 0)),
            pl.BlockSpec((CONV_TILES, HALO, LANES), lambda i: (0, jnp.maximum(i * per - 1, 0), 0)),
            pl.BlockSpec((None, CONV_TILES, CONV_WIDTH, LANES), lambda i: (l, 0, 0, 0)),
            tvec, tvec, tvec,
        ],
        out_specs=pl.BlockSpec((tr, CONV_DIM), lambda i: (i, 0)),
        out_shape=jax.ShapeDtypeStruct((m, CONV_DIM), BF16),
        scratch_shapes=[
            pltpu.VMEM((CONV_TILES, HALO + tr, LANES), F32),
            pltpu.VMEM((CONV_TILES, tr, LANES), F32),
        ],
        compiler_params=_params(("arbitrary",), 32),
        name="conv",
    )(u3, u3, w_tiles, tiled(b), tiled(lg), tiled(lb))


CONV_S_BATCH = 8


def _ln_swish_bf16(acc, lg, lb):
    mean = jnp.mean(acc, axis=-1, keepdims=True)
    xc = acc - mean
    var = jnp.mean(xc * xc, axis=-1, keepdims=True)
    y = xc * lax.rsqrt(var + EPS) * lg + lb
    return (y * jax.nn.sigmoid(y)).astype(BF16)


def _conv_s_kernel(u_ref, hist_ref, w_ref, b_ref, lg_ref, lb_ref, c_ref, nh_ref, up_ref, *, t):
    nhist = CONV_WIDTH - 1
    for bb in range(CONV_S_BATCH):
        up_ref[0:nhist, :] = hist_ref[bb]
        up_ref[nhist:nhist + t, :] = u_ref[bb]
        acc = jnp.broadcast_to(b_ref[...], (t, CONV_DIM))
        for j in range(CONV_WIDTH):
            acc = acc + w_ref[j:j + 1, :] * up_ref[j:j + t, :]
        c_ref[bb] = _ln_swish_bf16(acc, lg_ref[...], lb_ref[...])
        nh_ref[bb] = up_ref[t:t + nhist, :]


def _conv_s(u, hist, w, b, lg, lb, l):
    nb, t, _ = u.shape
    nhist = CONV_WIDTH - 1
    vec = pl.BlockSpec((None, 1, CONV_DIM), lambda i: (l, 0, 0))
    return pl.pallas_call(
        functools.partial(_conv_s_kernel, t=t),
        grid=(nb // CONV_S_BATCH,),
        in_specs=[
            pl.BlockSpec((CONV_S_BATCH, t, CONV_DIM), lambda i: (i, 0, 0)),
            pl.BlockSpec((None, CONV_S_BATCH, nhist, CONV_DIM), lambda i: (l, i, 0, 0)),
            pl.BlockSpec((None, CONV_WIDTH, CONV_DIM), lambda i: (l, 0, 0)),
            vec, vec, vec,
        ],
        out_specs=[
            pl.BlockSpec((CONV_S_BATCH, t, CONV_DIM), lambda i: (i, 0, 0)),
            pl.BlockSpec((CONV_S_BATCH, nhist, CONV_DIM), lambda i: (i, 0, 0)),
        ],
        out_shape=[
            jax.ShapeDtypeStruct((nb, t, CONV_DIM), BF16),
            jax.ShapeDtypeStruct((nb, nhist, CONV_DIM), F32),
        ],
        scratch_shapes=[pltpu.VMEM((nhist + t + 2, CONV_DIM), F32)],
        compiler_params=_params(("arbitrary",), 32),
        name="conv_s",
    )(u, hist, w, b, lg, lb)


ATT_Q = 2 * CHUNK
ATT_K = 4 * CHUNK
ATT_BLOCK = 512


def _sink_column(sink_ref, kh, rows_per_head):
    n = GROUP * rows_per_head
    row = lax.broadcasted_iota(jnp.int32, (n, 1), 0)
    col = jnp.full((n, 1), sink_ref[kh * GROUP + GROUP - 1], F32)
    for g in range(GROUP - 2, -1, -1):
        col = jnp.where(row < (g + 1) * rows_per_head, sink_ref[kh * GROUP + g], col)
    return col


def _stack_heads(q_ref, r0, rows, kh):
    parts = [q_ref[r0:r0 + rows, (kh * GROUP + g) * HEAD_DIM:(kh * GROUP + g + 1) * HEAD_DIM] for g in range(GROUP)]
    return jnp.concatenate(parts, axis=0)


def _attn_kernel(sink_ref, q_ref, k_ref, kh_ref, v_ref, vh_ref, o_ref, kbuf, vbuf):
    i = pl.program_id(0)
    kbuf[0:WINDOW, :] = kh_ref[...].astype(BF16)
    kbuf[WINDOW:, :] = k_ref[...].astype(BF16)
    vbuf[0:WINDOW, :] = vh_ref[...].astype(BF16)
    vbuf[WINDOW:, :] = v_ref[...].astype(BF16)

    n = GROUP * ATT_Q
    row_chunk = (lax.broadcasted_iota(jnp.int32, (n, ATT_K), 0) % ATT_Q) // CHUNK
    key_chunk = lax.broadcasted_iota(jnp.int32, (n, ATT_K), 1) // CHUNK
    band = (key_chunk >= row_chunk) & (key_chunk <= row_chunk + 2)
    band_first = band & ((key_chunk >= 2) | (i > 0))

    for t in range(ATT_BLOCK // ATT_Q):
        r0 = t * ATT_Q
        mask = band_first if t == 0 else band
        for kh in range(N_KV_HEADS):
            qs = _stack_heads(q_ref, r0, ATT_Q, kh)
            kk = kbuf[r0:r0 + ATT_K, kh * HEAD_DIM:(kh + 1) * HEAD_DIM]
            vv = vbuf[r0:r0 + ATT_K, kh * HEAD_DIM:(kh + 1) * HEAD_DIM]
            s = lax.dot_general(qs, kk, _NT, preferred_element_type=F32)
            s = jnp.where(mask, s, -jnp.inf)
            sink = _sink_column(sink_ref, kh, ATT_Q)
            mx = jnp.maximum(jnp.max(s, axis=-1, keepdims=True), sink)
            e = jnp.exp(s - mx)
            den = jnp.sum(e, axis=-1, keepdims=True) + jnp.exp(sink - mx)
            o = jnp.dot(e.astype(BF16), vv, preferred_element_type=F32) / den
            o = jnp.concatenate([o[g * ATT_Q:(g + 1) * ATT_Q] for g in range(GROUP)], axis=1)
            o_ref[r0:r0 + ATT_Q, kh * GROUP * HEAD_DIM:(kh + 1) * GROUP * HEAD_DIM] = o.astype(BF16)


def _attn(q, k, v, sinks):
    m = q.shape[0]
    per = ATT_BLOCK // WINDOW
    main = pl.BlockSpec((ATT_BLOCK, KV_DIM), lambda i: (i, 0))
    halo = pl.BlockSpec((WINDOW, KV_DIM), lambda i: (jnp.maximum(i * per - 1, 0), 0))
    return pl.pallas_call(
        _attn_kernel,
        grid=(m // ATT_BLOCK,),
        in_specs=[
            pl.BlockSpec(memory_space=pltpu.SMEM),
            pl.BlockSpec((ATT_BLOCK, ATTN_DIM), lambda i: (i, 0)),
            main, halo, main, halo,
        ],
        out_specs=pl.BlockSpec((ATT_BLOCK, ATTN_DIM), lambda i: (i, 0)),
        out_shape=jax.ShapeDtypeStruct((m, ATTN_DIM), BF16),
        scratch_shapes=[pltpu.VMEM((WINDOW + ATT_BLOCK, KV_DIM), BF16)] * 2,
        compiler_params=_params(("arbitrary",), 32),
        name="attn",
    )(sinks, q, k, k, v, v)


def _attn_s_kernel(sink_ref, q_ref, k_ref, v_ref, ck_ref, cv_ref, o_ref, nk_ref, nv_ref, *, t):
    keep = WINDOW - t
    nk_ref[0:keep, :] = ck_ref[t:, :]
    nk_ref[keep:, :] = k_ref[...]
    nv_ref[0:keep, :] = cv_ref[t:, :]
    nv_ref[keep:, :] = v_ref[...]
    for kh in range(N_KV_HEADS):
        cols = slice(kh * HEAD_DIM, (kh + 1) * HEAD_DIM)
        qs = _stack_heads(q_ref, 0, t, kh)
        kc = ck_ref[:, cols].astype(BF16)
        kn = k_ref[:, cols].astype(BF16)
        s1 = lax.dot_general(qs, kc, _NT, preferred_element_type=F32)
        s2 = lax.dot_general(qs, kn, _NT, preferred_element_type=F32)
        sink = _sink_column(sink_ref, kh, t)
        mx = jnp.maximum(jnp.maximum(jnp.max(s1, axis=-1, keepdims=True), jnp.max(s2, axis=-1, keepdims=True)), sink)
        e1 = jnp.exp(s1 - mx)
        e2 = jnp.exp(s2 - mx)
        den = jnp.sum(e1, axis=-1, keepdims=True) + jnp.sum(e2, axis=-1, keepdims=True) + jnp.exp(sink - mx)
        o = jnp.dot(e1.astype(BF16), cv_ref[:, cols].astype(BF16), preferred_element_type=F32)
        o = o + jnp.dot(e2.astype(BF16), v_ref[:, cols].astype(BF16), preferred_element_type=F32)
        o = o / den
        o = jnp.concatenate([o[g * t:(g + 1) * t] for g in range(GROUP)], axis=1)
        o_ref[:, kh * GROUP * HEAD_DIM:(kh + 1) * GROUP * HEAD_DIM] = o.astype(BF16)


def _attn_s(q, k, v, cache_k, cache_v, sinks, l, t):
    nb = q.shape[0] // t
    new = pl.BlockSpec((t, KV_DIM), lambda b: (b, 0))
    cache = pl.BlockSpec((None, None, WINDOW, KV_DIM), lambda b: (l, b, 0, 0))
    out_cache = pl.BlockSpec((None, WINDOW, KV_DIM), lambda b: (b, 0, 0))
    return pl.pallas_call(
        functools.partial(_attn_s_kernel, t=t),
        grid=(nb,),
        in_specs=[
            pl.BlockSpec(memory_space=pltpu.SMEM),
            pl.BlockSpec((t, ATTN_DIM), lambda b: (b, 0)),
            new, new, cache, cache,
        ],
        out_specs=[pl.BlockSpec((t, ATTN_DIM), lambda b: (b, 0)), out_cache, out_cache],
        out_shape=[
            jax.ShapeDtypeStruct((nb * t, ATTN_DIM), BF16),
            jax.ShapeDtypeStruct((nb, WINDOW, KV_DIM), F32),
            jax.ShapeDtypeStruct((nb, WINDOW, KV_DIM), F32),
        ],
        compiler_params=_params(("arbitrary",), 32),
        name="attn_s",
    )(sinks, q, k, v, cache_k, cache_v)


def _merge_kernel(hb_ref, c_ref, o_ref, wgc_ref, wga_ref, wco_ref, wao_ref, mix_ref):
    h = hb_ref[...]
    gc = jax.nn.sigmoid(jnp.dot(h, wgc_ref[...], preferred_element_type=F32))
    conv_out = jnp.dot(c_ref[...], wco_ref[...], preferred_element_type=F32)
    ga = jax.nn.sigmoid(jnp.dot(h, wga_ref[...], preferred_element_type=F32))
    attn_out = jnp.dot(o_ref[...], wao_ref[...], preferred_element_type=F32)
    mix_ref[...] = (gc * conv_out + ga * attn_out).astype(BF16)


def _merge(hb, c, o, wgc, wga, wco, wao, l):
    m = hb.shape[0]
    tm, tn = _row_tile(m), 256
    wide = pl.BlockSpec((None, D_MODEL, tn), lambda i, j: (l, 0, j))
    narrow = pl.BlockSpec((None, CONV_DIM, tn), lambda i, j: (l, 0, j))
    rows = pl.BlockSpec((tm, CONV_DIM), lambda i, j: (i, 0))
    return pl.pallas_call(
        _merge_kernel,
        grid=(m // tm, D_MODEL // tn),
        in_specs=[pl.BlockSpec((tm, D_MODEL), lambda i, j: (i, 0)), rows, rows, wide, wide, narrow, narrow],
        out_specs=pl.BlockSpec((tm, tn), lambda i, j: (i, j)),
        out_shape=jax.ShapeDtypeStruct((m, D_MODEL), BF16),
        compiler_params=_params(("arbitrary", "arbitrary"), 48),
        name="merge",
    )(hb, c, o, wgc, wga, wco, wao)


def _outproj_kernel(x_ref, mix_ref, w_ref, g_ref, y_ref, hb_ref):
    y = x_ref[...] + jnp.dot(mix_ref[...], w_ref[...], preferred_element_type=F32)
    y_ref[...] = y
    hb_ref[...] = _rms_to_bf16(y, g_ref[...])


def _outproj(x, mix, w, g, l):
    m = x.shape[0]
    tm = min(m, 512)
    row = lambda i: (i, 0)
    return pl.pallas_call(
        _outproj_kernel,
        grid=(m // tm,),
        in_specs=[
            pl.BlockSpec((tm, D_MODEL), row),
            pl.BlockSpec((tm, D_MODEL), row),
            pl.BlockSpec((None, D_MODEL, D_MODEL), lambda i: (l, 0, 0)),
            pl.BlockSpec((None, 1, D_MODEL), lambda i: (l, 0, 0)),
        ],
        out_specs=[pl.BlockSpec((tm, D_MODEL), row), pl.BlockSpec((tm, D_MODEL), row)],
        out_shape=[jax.ShapeDtypeStruct((m, D_MODEL), F32), jax.ShapeDtypeStruct((m, D_MODEL), BF16)],
        compiler_params=_params(("arbitrary",), 56),
        name="outproj",
    )(x, mix, w, g)


FFN_TF = 512
RES_TILES = D_MODEL // FFN_TF


def _ffn_kernel(hb_ref, x_ref, g_ref, wg_ref, wu_ref, wd_ref, y_ref, *hbn_ref):
    j = pl.program_id(1)

    @pl.when(j == 0)
    def _():
        y_ref[...] = jnp.zeros(y_ref.shape, F32)

    for jj in range(RES_TILES):
        @pl.when(j == jj)
        def _():
            y_ref[:, jj * FFN_TF:(jj + 1) * FFN_TF] += x_ref[...]

    h = hb_ref[...]
    gate = jnp.dot(h, wg_ref[...], preferred_element_type=F32)
    up = jnp.dot(h, wu_ref[...], preferred_element_type=F32)
    act = (gate * jax.nn.sigmoid(gate) * up).astype(BF16)
    y_ref[...] += jnp.dot(act, wd_ref[...], preferred_element_type=F32)

    if hbn_ref:
        @pl.when(j == pl.num_programs(1) - 1)
        def _():
            hbn_ref[0][...] = _rms_to_bf16(y_ref[...], g_ref[...])


def _ffn(hb, x, g_next, wg, wu, wd, l, emit_next):
    m = x.shape[0]
    tm, tf = _row_tile(m), FFN_TF
    row = lambda i, j: (i, 0)
    ln = min(l + 1, DEPTH - 1)
    out_specs = [pl.BlockSpec((tm, D_MODEL), row)]
    out_shape = [jax.ShapeDtypeStruct((m, D_MODEL), F32)]
    if emit_next:
        out_specs.append(pl.BlockSpec((tm, D_MODEL), row))
        out_shape.append(jax.ShapeDtypeStruct((m, D_MODEL), BF16))
    return pl.pallas_call(
        _ffn_kernel,
        grid=(m // tm, D_FF // tf),
        in_specs=[
            pl.BlockSpec((tm, D_MODEL), row),
            pl.BlockSpec((tm, tf), lambda i, j: (i, jnp.minimum(j, RES_TILES - 1))),
            pl.BlockSpec((None, 1, D_MODEL), lambda i, j: (ln, 0, 0)),
            pl.BlockSpec((None, D_MODEL, tf), lambda i, j: (l, 0, j)),
            pl.BlockSpec((None, D_MODEL, tf), lambda i, j: (l, 0, j)),
            pl.BlockSpec((None, tf, D_MODEL), lambda i, j: (l, j, 0)),
        ],
        out_specs=out_specs,
        out_shape=out_shape,
        compiler_params=_params(("arbitrary", "arbitrary"), 60),
        name="ffn",
    )(hb, x, g_next, wg, wu, wd)


def _rope_tables(pos):
    half = ROPE_DIM // 2
    inv_freq = ROPE_THETA ** (-jnp.arange(half, dtype=F32) * 2.0 / ROPE_DIM)
    ang = pos[:, None] * inv_freq[None, :]
    cos, sin = jnp.cos(ang), jnp.sin(ang)
    n = pos.shape[0]
    rest = HEAD_DIM - ROPE_DIM
    zeros_h = jnp.zeros((n, half), F32)
    cos_t = jnp.concatenate([cos, cos, jnp.ones((n, rest), F32)], axis=1)
    sa_t = jnp.concatenate([-sin, zeros_h, jnp.zeros((n, rest), F32)], axis=1)
    sb_t = jnp.concatenate([zeros_h, sin, jnp.zeros((n, rest), F32)], axis=1)
    reps = LANES // HEAD_DIM
    return tuple(jnp.tile(a, (1, reps)) for a in (cos_t, sa_t, sb_t))


def _head_pool_matrices():
    head = jnp.arange(QK_DIM, dtype=jnp.int32) // HEAD_DIM
    onehot = head[:, None] == jnp.arange(LANES, dtype=jnp.int32)[None, :]
    e = jnp.where(onehot, 1.0 / HEAD_DIM, 0.0).astype(BF16)
    et = jnp.where(onehot.T, 1.0, 0.0).astype(BF16)
    return e, et


def kernel(x_prompt, x_sample, state_conv, cache_k, cache_v, norm_mix_g, w_in, w_dw, b_dw, conv_ln_g,
           conv_ln_b, w_conv_out, q_norm_g, k_norm_g, sinks, w_attn_out, w_out, norm_ffn_g, w_gate_up, w_down):
    nb, t = x_sample.shape[0], x_sample.shape[1]
    seq = x_prompt.shape[1]

    def cut(w, a, b):
        return w[:, :, a:b].astype(BF16)

    c1 = CONV_DIM
    c2 = c1 + CONV_DIM
    c3 = c2 + QK_DIM + KV_DIM
    c4 = c3 + D_MODEL
    w_lin, w_gate, w_qkv = cut(w_in, 0, c1), cut(w_in, c1, c2), cut(w_in, c2, c3)
    w_gc, w_ga = cut(w_in, c3, c4), cut(w_in, c4, c4 + D_MODEL)
    w_ffg, w_ffu = cut(w_gate_up, 0, D_FF), cut(w_gate_up, D_FF, 2 * D_FF)
    w_co, w_ao, w_o, w_dn = (w.astype(BF16) for w in (w_conv_out, w_attn_out, w_out, w_down))

    g_mix = norm_mix_g.reshape(DEPTH, 1, D_MODEL)
    g_ffn = norm_ffn_g.reshape(DEPTH, 1, D_MODEL)
    b_conv = b_dw.reshape(DEPTH, 1, CONV_DIM)
    ln_g = conv_ln_g.reshape(DEPTH, 1, CONV_DIM)
    ln_b = conv_ln_b.reshape(DEPTH, 1, CONV_DIM)
    qk_w = jnp.concatenate([jnp.tile(q_norm_g, (1, N_Q_HEADS)), jnp.tile(k_norm_g, (1, N_KV_HEADS))], axis=1)
    qk_w = qk_w.reshape(DEPTH, 1, QK_DIM)
    e, et = _head_pool_matrices()

    tabs_p = _rope_tables(jnp.arange(seq, dtype=F32))
    tabs_s = _rope_tables(jnp.tile(jnp.arange(t, dtype=F32) + PAST_LEN, nb))
    ck = cache_k.reshape(DEPTH, nb, WINDOW, KV_DIM)
    cv = cache_v.reshape(DEPTH, nb, WINDOW, KV_DIM)

    def token_tail(x, hb, c, o, l):
        mix = _merge(hb, c, o, w_gc, w_ga, w_co, w_ao, l)
        x1, hb2 = _outproj(x, mix, w_o, g_ffn, l)
        out = _ffn(hb2, x1, g_mix, w_ffg, w_ffu, w_dn, l, l + 1 < DEPTH)
        return (out[0], out[1]) if l + 1 < DEPTH else (out[0], None)

    xp = x_prompt.reshape(seq, D_MODEL)
    xs = x_sample.reshape(nb * t, D_MODEL)
    hbp = _norm(xp, g_mix, 0)
    hbs = _norm(xs, g_mix, 0)
    hist_p, k_p, v_p, hist_s, k_s, v_s = [], [], [], [], [], []
    for l in range(DEPTH):
        u3 = _glu(hbp, w_lin, w_gate, l)
        q, k, v = _qkv(hbp, w_qkv, tabs_p, qk_w, e, et, l)
        c = _conv(u3, w_dw, b_conv, ln_g, ln_b, l)
        o = _attn(q, k, v, sinks[l])
        xp, hbp_next = token_tail(xp, hbp, c, o, l)
        hist_p.append(u3[:, seq - (CONV_WIDTH - 1):, :].transpose(1, 0, 2).reshape(1, CONV_WIDTH - 1, CONV_DIM))
        k_p.append(k[seq - WINDOW:].reshape(1, WINDOW, N_KV_HEADS, HEAD_DIM))
        v_p.append(v[seq - WINDOW:].reshape(1, WINDOW, N_KV_HEADS, HEAD_DIM))
        hbp = hbp_next

        u3 = _glu(hbs, w_lin, w_gate, l)
        q, k, v = _qkv(hbs, w_qkv, tabs_s, qk_w, e, et, l)
        u = u3.transpose(1, 0, 2).reshape(nb, t, CONV_DIM)
        c, nh = _conv_s(u, state_conv, w_dw, b_conv, ln_g, ln_b, l)
        o, nk, nv = _attn_s(q, k, v, ck, cv, sinks[l], l, t)
        xs, hbs_next = token_tail(xs, hbs, c.reshape(nb * t, CONV_DIM), o, l)
        hist_s.append(nh)
        k_s.append(nk.reshape(nb, WINDOW, N_KV_HEADS, HEAD_DIM))
        v_s.append(nv.reshape(nb, WINDOW, N_KV_HEADS, HEAD_DIM))
        hbs = hbs_next

    return (xp.reshape(x_prompt.shape), xs.reshape(x_sample.shape),
            jnp.stack(hist_p), jnp.stack(k_p), jnp.stack(v_p),
            jnp.stack(hist_s), jnp.stack(k_s), jnp.stack(v_s))
```

```python
import functools

import jax
import jax.numpy as jnp
from jax import lax
from jax.experimental import pallas as pl
from jax.experimental.pallas import tpu as pltpu

F32 = jnp.float32
BF16 = jnp.bfloat16

D_MODEL = 2048
DEPTH = 4
CHUNK = 64
CONV_DIM = D_MODEL // 2
CONV_WIDTH = 31
HEAD_DIM = 64
N_Q_HEADS = (D_MODEL // 2) // HEAD_DIM
N_KV_HEADS = 4
GROUP = N_Q_HEADS // N_KV_HEADS
ATTN_DIM = N_Q_HEADS * HEAD_DIM
KV_DIM = N_KV_HEADS * HEAD_DIM
QK_DIM = ATTN_DIM + KV_DIM
WINDOW = 128
ROPE_DIM = HEAD_DIM // 4
ROPE_THETA = 500000.0
D_FF = ((8 * D_MODEL + 3 * 256 - 1) // (3 * 256)) * 256
EPS = 1e-6
PAST_LEN = 4096
LOG2E = 1.4426950408889634
Q_SCALE = (HEAD_DIM ** -0.5) * LOG2E

LANES = 128
SUBLANES = 8
MIB = 1024 * 1024
CONV_TILES = CONV_DIM // LANES

_NT = (((1,), (1,)), ((), ()))


def _params(sem, vmem_mib):
    return pltpu.CompilerParams(dimension_semantics=sem, vmem_limit_bytes=vmem_mib * MIB)


def _row_tile(m):
    return min(m, 1024)


def _parts(rows, n):
    step = rows // n
    return [(p * step, (p + 1) * step) for p in range(n)]


def _dot(a, b):
    return jnp.dot(a, b, preferred_element_type=F32)


def _rms_to_bf16(x, g):
    ms = jnp.mean(x * x, axis=-1, keepdims=True)
    return (x * lax.rsqrt(ms + EPS) * g).astype(BF16)


def _norm_kernel(x_ref, g_ref, hb_ref):
    hb_ref[...] = _rms_to_bf16(x_ref[...], g_ref[...])


def _norm(x, g, l):
    m = x.shape[0]
    tm = min(m, 512)
    return pl.pallas_call(
        _norm_kernel,
        grid=(m // tm,),
        in_specs=[
            pl.BlockSpec((tm, D_MODEL), lambda i: (i, 0)),
            pl.BlockSpec((None, 1, D_MODEL), lambda i: (l, 0, 0)),
        ],
        out_specs=pl.BlockSpec((tm, D_MODEL), lambda i: (i, 0)),
        out_shape=jax.ShapeDtypeStruct((m, D_MODEL), BF16),
        compiler_params=_params(("arbitrary",), 32),
        name="norm",
    )(x, g)


GLU_TN = 512
GLU_PARTS = 4


def _glu_kernel(hb_ref, wl_ref, wg_ref, u_ref, *, tm):
    parts = _parts(tm, GLU_PARTS)
    ab = []
    for r0, r1 in parts:
        h = hb_ref[r0:r1, :]
        ab.append((_dot(h, wl_ref[...]), _dot(h, wg_ref[...])))
    for (r0, r1), (a, b) in zip(parts, ab):
        u = a * jax.nn.sigmoid(b)
        for t in range(GLU_TN // LANES):
            u_ref[t, r0:r1, :] = u[:, t * LANES:(t + 1) * LANES]


def _glu(hb, wl, wg, l):
    m = hb.shape[0]
    tm, tn = _row_tile(m), GLU_TN
    return pl.pallas_call(
        functools.partial(_glu_kernel, tm=tm),
        grid=(m // tm, CONV_DIM // tn),
        in_specs=[
            pl.BlockSpec((tm, D_MODEL), lambda i, j: (i, 0)),
            pl.BlockSpec((None, D_MODEL, tn), lambda i, j: (l, 0, j)),
            pl.BlockSpec((None, D_MODEL, tn), lambda i, j: (l, 0, j)),
        ],
        out_specs=pl.BlockSpec((tn // LANES, tm, LANES), lambda i, j: (j, i, 0)),
        out_shape=jax.ShapeDtypeStruct((CONV_TILES, m, LANES), F32),
        compiler_params=_params(("arbitrary", "arbitrary"), 48),
        name="glu",
    )(hb, wl, wg)


QKV_PARTS = 2


def _split_bf16(a):
    hi = a.astype(BF16)
    lo = (a - hi.astype(F32)).astype(BF16)
    return hi, lo


def _qkv_kernel(hb_ref, w_ref, cos_ref, sa_ref, sb_ref, nw_ref, e_ref, et_ref, q_ref, k_ref, v_ref, *, tm):
    parts = _parts(tm, QKV_PARTS)
    zs = [_dot(hb_ref[r0:r1, :], w_ref[...]) for r0, r1 in parts]
    for (r0, r1), z in zip(parts, zs):
        v_ref[r0:r1, :] = z[:, QK_DIM:]
        qk = z[:, :QK_DIM]
        hi, lo = _split_bf16(qk * qk)
        e = e_ref[...]
        ms = _dot(hi, e) + _dot(lo, e)
        rhi, rlo = _split_bf16(lax.rsqrt(ms + EPS))
        et = et_ref[...]
        scale = _dot(rhi, et) + _dot(rlo, et)
        y = qk * scale * nw_ref[...]
        cos, sa, sb = cos_ref[r0:r1, :], sa_ref[r0:r1, :], sb_ref[r0:r1, :]
        for cg in range(QK_DIM // LANES):
            yc = y[:, cg * LANES:(cg + 1) * LANES]
            rot = yc * cos + pltpu.roll(yc, LANES - ROPE_DIM // 2, 1) * sa + pltpu.roll(yc, ROPE_DIM // 2, 1) * sb
            if cg < ATTN_DIM // LANES:
                q_ref[r0:r1, cg * LANES:(cg + 1) * LANES] = (rot * Q_SCALE).astype(BF16)
            else:
                c0 = cg * LANES - ATTN_DIM
                k_ref[r0:r1, c0:c0 + LANES] = rot


def _qkv(hb, w, tabs, nw, e, et, l):
    m = hb.shape[0]
    tm = min(m, 512)
    cos, sa, sb = tabs
    tab_spec = pl.BlockSpec((tm, LANES), lambda i: (i, 0))
    return pl.pallas_call(
        functools.partial(_qkv_kernel, tm=tm),
        grid=(m // tm,),
        in_specs=[
            pl.BlockSpec((tm, D_MODEL), lambda i: (i, 0)),
            pl.BlockSpec((None, D_MODEL, QK_DIM + KV_DIM), lambda i: (l, 0, 0)),
            tab_spec, tab_spec, tab_spec,
            pl.BlockSpec((None, 1, QK_DIM), lambda i: (l, 0, 0)),
            pl.BlockSpec((QK_DIM, LANES), lambda i: (0, 0)),
            pl.BlockSpec((LANES, QK_DIM), lambda i: (0, 0)),
        ],
        out_specs=[
            pl.BlockSpec((tm, ATTN_DIM), lambda i: (i, 0)),
            pl.BlockSpec((tm, KV_DIM), lambda i: (i, 0)),
            pl.BlockSpec((tm, KV_DIM), lambda i: (i, 0)),
        ],
        out_shape=[
            jax.ShapeDtypeStruct((m, ATTN_DIM), BF16),
            jax.ShapeDtypeStruct((m, KV_DIM), F32),
            jax.ShapeDtypeStruct((m, KV_DIM), F32),
        ],
        compiler_params=_params(("arbitrary",), 56),
        name="qkv",
    )(hb, w, cos, sa, sb, nw, e, et)


CONV_BLOCK = 256
CONV_ROWS = 64
LN_ROWS = 32
HALO = 32


def _conv_kernel(u_ref, halo_ref, w_ref, b_ref, lg_ref, lb_ref, c_ref, up_ref, y_ref):
    i = pl.program_id(0)

    @pl.when(i == 0)
    def _():
        up_ref[:, 0:HALO, :] = jnp.zeros((CONV_TILES, HALO, LANES), F32)

    @pl.when(i > 0)
    def _():
        up_ref[:, 0:HALO, :] = halo_ref[...]

    up_ref[:, HALO:, :] = u_ref[...]
    first = HALO - (CONV_WIDTH - 1)
    groups = CONV_ROWS // SUBLANES
    for ct in range(CONV_TILES):
        taps = [w_ref[ct, pl.ds(j, SUBLANES, stride=0), :] for j in range(CONV_WIDTH)]
        bias = b_ref[ct, pl.ds(0, SUBLANES, stride=0), :]
        for rc in range(CONV_BLOCK // CONV_ROWS):
            acc = jnp.broadcast_to(bias, (groups, SUBLANES, LANES))
            for j in range(CONV_WIDTH):
                s = rc * CONV_ROWS + first + j
                acc = acc + taps[j] * up_ref[ct, s:s + CONV_ROWS, :].reshape(groups, SUBLANES, LANES)
            y_ref[ct, rc * CONV_ROWS:(rc + 1) * CONV_ROWS, :] = acc.reshape(CONV_ROWS, LANES)

    lg, lb = lg_ref[...], lb_ref[...]
    for r in range(CONV_BLOCK // LN_ROWS):
        r0 = r * LN_ROWS
        y = y_ref[:, r0:r0 + LN_ROWS, :]
        mean = jnp.sum(jnp.sum(y, axis=0, keepdims=True), axis=2, keepdims=True) * (1.0 / CONV_DIM)
        yc = y - mean
        var = jnp.sum(jnp.sum(yc * yc, axis=0, keepdims=True), axis=2, keepdims=True) * (1.0 / CONV_DIM)
        z = yc * lax.rsqrt(var + EPS) * lg + lb
        z = (z * jax.nn.sigmoid(z)).astype(BF16)
        for ct in range(CONV_TILES):
            c_ref[r0:r0 + LN_ROWS, ct * LANES:(ct + 1) * LANES] = z[ct]


def _conv(u3, w, b, lg, lb, l):
    m = u3.shape[1]
    tr = CONV_BLOCK
    per = tr // HALO
    tiled = lambda a: a.reshape(DEPTH, CONV_TILES, 1, LANES)
    w_tiles = w.reshape(DEPTH, CONV_WIDTH, CONV_TILES, LANES).transpose(0, 2, 1, 3)
    tvec = pl.BlockSpec((None, CONV_TILES, 1, LANES), lambda i: (l, 0, 0, 0))
    return pl.pallas_call(
        _conv_kernel,
        grid=(m // tr,),
        in_specs=[
            pl.BlockSpec((CONV_TILES, tr, LANES), lambda i: (0, i, 0)),
            pl.BlockSpec((CONV_TILES, HALO, LANES), lambda i: (0, jnp.maximum(i * per - 1, 0), 0)),
            pl.BlockSpec((None, CONV_TILES, CONV_WIDTH, LANES), lambda i: (l, 0, 0, 0)),
            tvec, tvec, tvec,
        ],
        out_specs=pl.BlockSpec((tr, CONV_DIM), lambda i: (i, 0)),
        out_shape=jax.ShapeDtypeStruct((m, CONV_DIM), BF16),
        scratch_shapes=[
            pltpu.VMEM((CONV_TILES, HALO + tr, LANES), F32),
            pltpu.VMEM((CONV_TILES, tr, LANES), F32),
        ],
        compiler_params=_params(("arbitrary",), 32),
        name="conv",
    )(u3, u3, w_tiles, tiled(b), tiled(lg), tiled(lb))


CONV_S_BATCH = 8


def _ln_swish_bf16(acc, lg, lb):
    mean = jnp.mean(acc, axis=-1, keepdims=True)
    xc = acc - mean
    var = jnp.mean(xc * xc, axis=-1, keepdims=True)
    y = xc * lax.rsqrt(var + EPS) * lg + lb
    return (y * jax.nn.sigmoid(y)).astype(BF16)


def _conv_s_kernel(u_ref, hist_ref, w_ref, b_ref, lg_ref, lb_ref, c_ref, nh_ref, up_ref, *, t):
    nhist = CONV_WIDTH - 1
    for bb in range(CONV_S_BATCH):
        up_ref[0:nhist, :] = hist_ref[bb]
        up_ref[nhist:nhist + t, :] = u_ref[bb]
        acc = jnp.broadcast_to(b_ref[...], (t, CONV_DIM))
        for j in range(CONV_WIDTH):
            acc = acc + w_ref[j:j + 1, :] * up_ref[j:j + t, :]
        c_ref[bb] = _ln_swish_bf16(acc, lg_ref[...], lb_ref[...])
        nh_ref[bb] = up_ref[t:t + nhist, :]


def _conv_s(u, hist, w, b, lg, lb, l):
    nb, t, _ = u.shape
    nhist = CONV_WIDTH - 1
    vec = pl.BlockSpec((None, 1, CONV_DIM), lambda i: (l, 0, 0))
    return pl.pallas_call(
        functools.partial(_conv_s_kernel, t=t),
        grid=(nb // CONV_S_BATCH,),
        in_specs=[
            pl.BlockSpec((CONV_S_BATCH, t, CONV_DIM), lambda i: (i, 0, 0)),
            pl.BlockSpec((None, CONV_S_BATCH, nhist, CONV_DIM), lambda i: (l, i, 0, 0)),
            pl.BlockSpec((None, CONV_WIDTH, CONV_DIM), lambda i: (l, 0, 0)),
            vec, vec, vec,
        ],
        out_specs=[
            pl.BlockSpec((CONV_S_BATCH, t, CONV_DIM), lambda i: (i, 0, 0)),
            pl.BlockSpec((CONV_S_BATCH, nhist, CONV_DIM), lambda i: (i, 0, 0)),
        ],
        out_shape=[
            jax.ShapeDtypeStruct((nb, t, CONV_DIM), BF16),
            jax.ShapeDtypeStruct((nb, nhist, CONV_DIM), F32),
        ],
        scratch_shapes=[pltpu.VMEM((nhist + t + 2, CONV_DIM), F32)],
        compiler_params=_params(("arbitrary",), 32),
        name="conv_s",
    )(u, hist, w, b, lg, lb)


ATT_Q = 2 * CHUNK
ATT_K = 4 * CHUNK
ATT_BLOCK = 512
V_ROWS = HEAD_DIM + 16


def _sink_row(sink_ref, kh, cols_per_head):
    n = GROUP * cols_per_head
    lane_head = lax.broadcasted_iota(jnp.int32, (1, n), 1) // cols_per_head
    sink = jnp.full((1, n), sink_ref[kh * GROUP + GROUP - 1], F32)
    for g in range(GROUP - 2, -1, -1):
        sink = jnp.where(lane_head <= g, sink_ref[kh * GROUP + g], sink)
    return sink * LOG2E


def _attn_kernel(sink_ref, q_ref, k_ref, kh_ref, v_ref, vh_ref, o_ref, kbuf, vtbuf, otbuf):
    i = pl.program_id(0)
    kbuf[0:WINDOW, :] = kh_ref[...].astype(BF16)
    kbuf[WINDOW:, :] = k_ref[...].astype(BF16)
    vt_h = vh_ref[...].T.astype(BF16)
    vt_m = v_ref[...].T.astype(BF16)
    for kh in range(N_KV_HEADS):
        vtbuf[kh * V_ROWS:kh * V_ROWS + HEAD_DIM, 0:WINDOW] = vt_h[kh * HEAD_DIM:(kh + 1) * HEAD_DIM]
        vtbuf[kh * V_ROWS:kh * V_ROWS + HEAD_DIM, WINDOW:] = vt_m[kh * HEAD_DIM:(kh + 1) * HEAD_DIM]
        vtbuf[kh * V_ROWS + HEAD_DIM:(kh + 1) * V_ROWS, :] = jnp.ones((V_ROWS - HEAD_DIM, WINDOW + ATT_BLOCK), BF16)
    qt = q_ref[...].T

    n = GROUP * ATT_Q
    lane = lax.broadcasted_iota(jnp.int32, (CHUNK, n), 1)
    first_chunk_rows = (lane % ATT_Q) < CHUNK
    neg = -jnp.inf
    sinks = [_sink_row(sink_ref, kh, ATT_Q) for kh in range(N_KV_HEADS)]

    for t in range(ATT_BLOCK // ATT_Q):
        r0 = t * ATT_Q
        sts = []
        for kh in range(N_KV_HEADS):
            kk = kbuf[r0:r0 + ATT_K, kh * HEAD_DIM:(kh + 1) * HEAD_DIM]
            qtt = jnp.concatenate(
                [qt[(kh * GROUP + g) * HEAD_DIM:(kh * GROUP + g + 1) * HEAD_DIM, r0:r0 + ATT_Q] for g in range(GROUP)],
                axis=1)
            sts.append(_dot(kk, qtt))
        es, mxs = [], []
        for kh in range(N_KV_HEADS):
            st = sts[kh]
            c0 = jnp.where(first_chunk_rows, st[0:CHUNK], neg)
            c1 = st[CHUNK:2 * CHUNK]
            c3 = jnp.where(first_chunk_rows, neg, st[3 * CHUNK:])
            if t == 0:
                c0 = jnp.where(i > 0, c0, neg)
                c1 = jnp.where(i > 0, c1, neg)
            st = jnp.concatenate([c0, c1, st[2 * CHUNK:3 * CHUNK], c3], axis=0)
            mx = jnp.maximum(jnp.max(st, axis=0, keepdims=True), sinks[kh])
            es.append(jnp.exp2(st - mx).astype(BF16))
            mxs.append(mx)
        for kh in range(N_KV_HEADS):
            vt = vtbuf[kh * V_ROWS:(kh + 1) * V_ROWS, r0:r0 + ATT_K]
            ot = _dot(vt, es[kh])
            den = ot[HEAD_DIM:HEAD_DIM + 1] + jnp.exp2(sinks[kh] - mxs[kh])
            ot = ot[0:HEAD_DIM] / den
            for g in range(GROUP):
                h = kh * GROUP + g
                otbuf[h * HEAD_DIM:(h + 1) * HEAD_DIM, :] = ot[:, g * ATT_Q:(g + 1) * ATT_Q]
        o_ref[r0:r0 + ATT_Q, :] = otbuf[...].T.astype(BF16)


def _attn(q, k, v, sinks):
    m = q.shape[0]
    per = ATT_BLOCK // WINDOW
    main = pl.BlockSpec((ATT_BLOCK, KV_DIM), lambda i: (i, 0))
    halo = pl.BlockSpec((WINDOW, KV_DIM), lambda i: (jnp.maximum(i * per - 1, 0), 0))
    return pl.pallas_call(
        _attn_kernel,
        grid=(m // ATT_BLOCK,),
        in_specs=[
            pl.BlockSpec(memory_space=pltpu.SMEM),
            pl.BlockSpec((ATT_BLOCK, ATTN_DIM), lambda i: (i, 0)),
            main, halo, main, halo,
        ],
        out_specs=pl.BlockSpec((ATT_BLOCK, ATTN_DIM), lambda i: (i, 0)),
        out_shape=jax.ShapeDtypeStruct((m, ATTN_DIM), BF16),
        scratch_shapes=[
            pltpu.VMEM((WINDOW + ATT_BLOCK, KV_DIM), BF16),
            pltpu.VMEM((N_KV_HEADS * V_ROWS, WINDOW + ATT_BLOCK), BF16),
            pltpu.VMEM((ATTN_DIM, ATT_Q), F32),
        ],
        compiler_params=_params(("arbitrary",), 32),
        name="attn",
    )(sinks, q, k, k, v, v)


ATT_S_BATCH = 4


def _stack_heads(q_ref, r0, rows, kh):
    parts = [q_ref[r0:r0 + rows, (kh * GROUP + g) * HEAD_DIM:(kh * GROUP + g + 1) * HEAD_DIM] for g in range(GROUP)]
    return jnp.concatenate(parts, axis=0)


def _sink_column(sink_ref, kh, rows_per_head):
    n = GROUP * rows_per_head
    row = lax.broadcasted_iota(jnp.int32, (n, 1), 0)
    col = jnp.full((n, 1), sink_ref[kh * GROUP + GROUP - 1], F32)
    for g in range(GROUP - 2, -1, -1):
        col = jnp.where(row < (g + 1) * rows_per_head, sink_ref[kh * GROUP + g], col)
    return col * LOG2E


def _attn_s_kernel(sink_ref, q_ref, k_ref, v_ref, ck_ref, cv_ref, o_ref, nk_ref, nv_ref, *, t):
    keep = WINDOW - t
    sinks = [_sink_column(sink_ref, kh, t) for kh in range(N_KV_HEADS)]
    scores = []
    for bb in range(ATT_S_BATCH):
        r0 = bb * t
        nk_ref[bb, 0:keep, :] = ck_ref[bb, t:, :]
        nk_ref[bb, keep:, :] = k_ref[r0:r0 + t, :]
        nv_ref[bb, 0:keep, :] = cv_ref[bb, t:, :]
        nv_ref[bb, keep:, :] = v_ref[r0:r0 + t, :]
        for kh in range(N_KV_HEADS):
            cols = slice(kh * HEAD_DIM, (kh + 1) * HEAD_DIM)
            qs = _stack_heads(q_ref, r0, t, kh)
            s1 = lax.dot_general(qs, ck_ref[bb, :, cols].astype(BF16), _NT, preferred_element_type=F32)
            s2 = lax.dot_general(qs, k_ref[r0:r0 + t, cols].astype(BF16), _NT, preferred_element_type=F32)
            scores.append((s1, s2))
    probs = []
    for bb in range(ATT_S_BATCH):
        for kh in range(N_KV_HEADS):
            s1, s2 = scores[bb * N_KV_HEADS + kh]
            sink = sinks[kh]
            mx = jnp.maximum(jnp.maximum(jnp.max(s1, axis=-1, keepdims=True), jnp.max(s2, axis=-1, keepdims=True)), sink)
            e1 = jnp.exp2(s1 - mx)
            e2 = jnp.exp2(s2 - mx)
            den = jnp.sum(e1, axis=-1, keepdims=True) + jnp.sum(e2, axis=-1, keepdims=True) + jnp.exp2(sink - mx)
            probs.append((e1.astype(BF16), e2.astype(BF16), den))
    for bb in range(ATT_S_BATCH):
        r0 = bb * t
        for kh in range(N_KV_HEADS):
            cols = slice(kh * HEAD_DIM, (kh + 1) * HEAD_DIM)
            e1, e2, den = probs[bb * N_KV_HEADS + kh]
            o = _dot(e1, cv_ref[bb, :, cols].astype(BF16)) + _dot(e2, v_ref[r0:r0 + t, cols].astype(BF16))
            o = o / den
            o = jnp.concatenate([o[g * t:(g + 1) * t] for g in range(GROUP)], axis=1)
            o_ref[r0:r0 + t, kh * GROUP * HEAD_DIM:(kh + 1) * GROUP * HEAD_DIM] = o.astype(BF16)


def _attn_s(q, k, v, cache_k, cache_v, sinks, l, t):
    nb = q.shape[0] // t
    rows = ATT_S_BATCH * t
    new = pl.BlockSpec((rows, KV_DIM), lambda b: (b, 0))
    cache = pl.BlockSpec((None, ATT_S_BATCH, WINDOW, KV_DIM), lambda b: (l, b, 0, 0))
    out_cache = pl.BlockSpec((ATT_S_BATCH, WINDOW, KV_DIM), lambda b: (b, 0, 0))
    return pl.pallas_call(
        functools.partial(_attn_s_kernel, t=t),
        grid=(nb // ATT_S_BATCH,),
        in_specs=[
            pl.BlockSpec(memory_space=pltpu.SMEM),
            pl.BlockSpec((rows, ATTN_DIM), lambda b: (b, 0)),
            new, new, cache, cache,
        ],
        out_specs=[pl.BlockSpec((rows, ATTN_DIM), lambda b: (b, 0)), out_cache, out_cache],
        out_shape=[
            jax.ShapeDtypeStruct((nb * t, ATTN_DIM), BF16),
            jax.ShapeDtypeStruct((nb, WINDOW, KV_DIM), F32),
            jax.ShapeDtypeStruct((nb, WINDOW, KV_DIM), F32),
        ],
        compiler_params=_params(("arbitrary",), 32),
        name="attn_s",
    )(sinks, q, k, v, cache_k, cache_v)


MERGE_PARTS = 1


def _merge_kernel(hb_ref, c_ref, o_ref, wgc_ref, wga_ref, wco_ref, wao_ref, mix_ref, *, tm):
    parts = _parts(tm, MERGE_PARTS)
    prods = []
    for r0, r1 in parts:
        h = hb_ref[r0:r1, :]
        prods.append((_dot(h, wgc_ref[...]), _dot(c_ref[r0:r1, :], wco_ref[...]),
                      _dot(h, wga_ref[...]), _dot(o_ref[r0:r1, :], wao_ref[...])))
    for (r0, r1), (gc, conv_out, ga, attn_out) in zip(parts, prods):
        mix = jax.nn.sigmoid(gc) * conv_out + jax.nn.sigmoid(ga) * attn_out
        mix_ref[r0:r1, :] = mix.astype(BF16)


def _merge(hb, c, o, wgc, wga, wco, wao, l):
    m = hb.shape[0]
    tm, tn = _row_tile(m), 512
    wide = pl.BlockSpec((None, D_MODEL, tn), lambda i, j: (l, 0, j))
    narrow = pl.BlockSpec((None, CONV_DIM, tn), lambda i, j: (l, 0, j))
    rows = pl.BlockSpec((tm, CONV_DIM), lambda i, j: (i, 0))
    return pl.pallas_call(
        functools.partial(_merge_kernel, tm=tm),
        grid=(m // tm, D_MODEL // tn),
        in_specs=[pl.BlockSpec((tm, D_MODEL), lambda i, j: (i, 0)), rows, rows, wide, wide, narrow, narrow],
        out_specs=pl.BlockSpec((tm, tn), lambda i, j: (i, j)),
        out_shape=jax.ShapeDtypeStruct((m, D_MODEL), BF16),
        compiler_params=_params(("arbitrary", "arbitrary"), 48),
        name="merge",
    )(hb, c, o, wgc, wga, wco, wao)


OUT_PARTS = 2


def _outproj_kernel(x_ref, mix_ref, w_ref, g_ref, y_ref, hb_ref, *, tm):
    parts = _parts(tm, OUT_PARTS)
    prods = [_dot(mix_ref[r0:r1, :], w_ref[...]) for r0, r1 in parts]
    for (r0, r1), p in zip(parts, prods):
        y = x_ref[r0:r1, :] + p
        y_ref[r0:r1, :] = y
        hb_ref[r0:r1, :] = _rms_to_bf16(y, g_ref[...])


def _outproj(x, mix, w, g, l):
    m = x.shape[0]
    tm = min(m, 512)
    row = lambda i: (i, 0)
    return pl.pallas_call(
        functools.partial(_outproj_kernel, tm=tm),
        grid=(m // tm,),
        in_specs=[
            pl.BlockSpec((tm, D_MODEL), row),
            pl.BlockSpec((tm, D_MODEL), row),
            pl.BlockSpec((None, D_MODEL, D_MODEL), lambda i: (l, 0, 0)),
            pl.BlockSpec((None, 1, D_MODEL), lambda i: (l, 0, 0)),
        ],
        out_specs=[pl.BlockSpec((tm, D_MODEL), row), pl.BlockSpec((tm, D_MODEL), row)],
        out_shape=[jax.ShapeDtypeStruct((m, D_MODEL), F32), jax.ShapeDtypeStruct((m, D_MODEL), BF16)],
        compiler_params=_params(("arbitrary",), 56),
        name="outproj",
    )(x, mix, w, g)


FFN_TF = 512
FFN_PARTS = 1
RES_TILES = D_MODEL // FFN_TF


def _ffn_kernel(hb_ref, x_ref, g_ref, wg_ref, wu_ref, wd_ref, y_ref, *hbn_ref, tm):
    j = pl.program_id(1)

    @pl.when(j == 0)
    def _():
        y_ref[...] = jnp.zeros(y_ref.shape, F32)

    for jj in range(RES_TILES):
        @pl.when(j == jj)
        def _():
            y_ref[:, jj * FFN_TF:(jj + 1) * FFN_TF] += x_ref[...]

    parts = _parts(tm, FFN_PARTS)
    gu = []
    for r0, r1 in parts:
        h = hb_ref[r0:r1, :]
        gu.append((_dot(h, wg_ref[...]), _dot(h, wu_ref[...])))
    for (r0, r1), (gate, up) in zip(parts, gu):
        act = (gate * jax.nn.sigmoid(gate) * up).astype(BF16)
        y_ref[r0:r1, :] += _dot(act, wd_ref[...])

    if hbn_ref:
        @pl.when(j == pl.num_programs(1) - 1)
        def _():
            hbn_ref[0][...] = _rms_to_bf16(y_ref[...], g_ref[...])


def _ffn(hb, x, g_next, wg, wu, wd, l, emit_next):
    m = x.shape[0]
    tm, tf = _row_tile(m), FFN_TF
    row = lambda i, j: (i, 0)
    ln = min(l + 1, DEPTH - 1)
    out_specs = [pl.BlockSpec((tm, D_MODEL), row)]
    out_shape = [jax.ShapeDtypeStruct((m, D_MODEL), F32)]
    if emit_next:
        out_specs.append(pl.BlockSpec((tm, D_MODEL), row))
        out_shape.append(jax.ShapeDtypeStruct((m, D_MODEL), BF16))
    return pl.pallas_call(
        functools.partial(_ffn_kernel, tm=tm),
        grid=(m // tm, D_FF // tf),
        in_specs=[
            pl.BlockSpec((tm, D_MODEL), row),
            pl.BlockSpec((tm, tf), lambda i, j: (i, jnp.minimum(j, RES_TILES - 1))),
            pl.BlockSpec((None, 1, D_MODEL), lambda i, j: (ln, 0, 0)),
            pl.BlockSpec((None, D_MODEL, tf), lambda i, j: (l, 0, j)),
            pl.BlockSpec((None, D_MODEL, tf), lambda i, j: (l, 0, j)),
            pl.BlockSpec((None, tf, D_MODEL), lambda i, j: (l, j, 0)),
        ],
        out_specs=out_specs,
        out_shape=out_shape,
        compiler_params=_params(("arbitrary", "arbitrary"), 60),
        name="ffn",
    )(hb, x, g_next, wg, wu, wd)


def _rope_tables(pos):
    half = ROPE_DIM // 2
    inv_freq = ROPE_THETA ** (-jnp.arange(half, dtype=F32) * 2.0 / ROPE_DIM)
    ang = pos[:, None] * inv_freq[None, :]
    cos, sin = jnp.cos(ang), jnp.sin(ang)
    n = pos.shape[0]
    rest = HEAD_DIM - ROPE_DIM
    zeros_h = jnp.zeros((n, half), F32)
    cos_t = jnp.concatenate([cos, cos, jnp.ones((n, rest), F32)], axis=1)
    sa_t = jnp.concatenate([-sin, zeros_h, jnp.zeros((n, rest), F32)], axis=1)
    sb_t = jnp.concatenate([zeros_h, sin, jnp.zeros((n, rest), F32)], axis=1)
    reps = LANES // HEAD_DIM
    return tuple(jnp.tile(a, (1, reps)) for a in (cos_t, sa_t, sb_t))


def _head_pool_matrices():
    head = jnp.arange(QK_DIM, dtype=jnp.int32) // HEAD_DIM
    onehot = head[:, None] == jnp.arange(LANES, dtype=jnp.int32)[None, :]
    e = jnp.where(onehot, 1.0 / HEAD_DIM, 0.0).astype(BF16)
    et = jnp.where(onehot.T, 1.0, 0.0).astype(BF16)
    return e, et


def kernel(x_prompt, x_sample, state_conv, cache_k, cache_v, norm_mix_g, w_in, w_dw, b_dw, conv_ln_g,
           conv_ln_b, w_conv_out, q_norm_g, k_norm_g, sinks, w_attn_out, w_out, norm_ffn_g, w_gate_up, w_down):
    nb, t = x_sample.shape[0], x_sample.shape[1]
    seq = x_prompt.shape[1]

    def cut(w, a, b):
        return w[:, :, a:b].astype(BF16)

    c1 = CONV_DIM
    c2 = c1 + CONV_DIM
    c3 = c2 + QK_DIM + KV_DIM
    c4 = c3 + D_MODEL
    w_lin, w_gate, w_qkv = cut(w_in, 0, c1), cut(w_in, c1, c2), cut(w_in, c2, c3)
    w_gc, w_ga = cut(w_in, c3, c4), cut(w_in, c4, c4 + D_MODEL)
    w_ffg, w_ffu = cut(w_gate_up, 0, D_FF), cut(w_gate_up, D_FF, 2 * D_FF)
    w_co, w_ao, w_o, w_dn = (w.astype(BF16) for w in (w_conv_out, w_attn_out, w_out, w_down))

    g_mix = norm_mix_g.reshape(DEPTH, 1, D_MODEL)
    g_ffn = norm_ffn_g.reshape(DEPTH, 1, D_MODEL)
    b_conv = b_dw.reshape(DEPTH, 1, CONV_DIM)
    ln_g = conv_ln_g.reshape(DEPTH, 1, CONV_DIM)
    ln_b = conv_ln_b.reshape(DEPTH, 1, CONV_DIM)
    qk_w = jnp.concatenate([jnp.tile(q_norm_g, (1, N_Q_HEADS)), jnp.tile(k_norm_g, (1, N_KV_HEADS))], axis=1)
    qk_w = qk_w.reshape(DEPTH, 1, QK_DIM)
    e, et = _head_pool_matrices()

    tabs_p = _rope_tables(jnp.arange(seq, dtype=F32))
    tabs_s = _rope_tables(jnp.tile(jnp.arange(t, dtype=F32) + PAST_LEN, nb))
    ck = cache_k.reshape(DEPTH, nb, WINDOW, KV_DIM)
    cv = cache_v.reshape(DEPTH, nb, WINDOW, KV_DIM)

    def token_tail(x, hb, c, o, l):
        mix = _merge(hb, c, o, w_gc, w_ga, w_co, w_ao, l)
        x1, hb2 = _outproj(x, mix, w_o, g_ffn, l)
        out = _ffn(hb2, x1, g_mix, w_ffg, w_ffu, w_dn, l, l + 1 < DEPTH)
        return (out[0], out[1]) if l + 1 < DEPTH else (out[0], None)

    xp = x_prompt.reshape(seq, D_MODEL)
    xs = x_sample.reshape(nb * t, D_MODEL)
    hbp = _norm(xp, g_mix, 0)
    hbs = _norm(xs, g_mix, 0)
    hist_p, k_p, v_p, hist_s, k_s, v_s = [], [], [], [], [], []
    for l in range(DEPTH):
        u3 = _glu(hbp, w_lin, w_gate, l)
        q, k, v = _qkv(hbp, w_qkv, tabs_p, qk_w, e, et, l)
        c = _conv(u3, w_dw, b_conv, ln_g, ln_b, l)
        o = _attn(q, k, v, sinks[l])
        xp, hbp_next = token_tail(xp, hbp, c, o, l)
        hist_p.append(u3[:, seq - (CONV_WIDTH - 1):, :].transpose(1, 0, 2).reshape(1, CONV_WIDTH - 1, CONV_DIM))
        k_p.append(k[seq - WINDOW:].reshape(1, WINDOW, N_KV_HEADS, HEAD_DIM))
        v_p.append(v[seq - WINDOW:].reshape(1, WINDOW, N_KV_HEADS, HEAD_DIM))
        hbp = hbp_next

        u3 = _glu(hbs, w_lin, w_gate, l)
        q, k, v = _qkv(hbs, w_qkv, tabs_s, qk_w, e, et, l)
        u = u3.transpose(1, 0, 2).reshape(nb, t, CONV_DIM)
        c, nh = _conv_s(u, state_conv, w_dw, b_conv, ln_g, ln_b, l)
        o, nk, nv = _attn_s(q, k, v, ck, cv, sinks[l], l, t)
        xs, hbs_next = token_tail(xs, hbs, c.reshape(nb * t, CONV_DIM), o, l)
        hist_s.append(nh)
        k_s.append(nk.reshape(nb, WINDOW, N_KV_HEADS, HEAD_DIM))
        v_s.append(nv.reshape(nb, WINDOW, N_KV_HEADS, HEAD_DIM))
        hbs = hbs_next

    return (xp.reshape(x_prompt.shape), xs.reshape(x_sample.shape),
            jnp.stack(hist_p), jnp.stack(k_p), jnp.stack(v_p),
            jnp.stack(hist_s), jnp.stack(k_s), jnp.stack(v_s))
```

```python
import functools

import jax
import jax.numpy as jnp
from jax import lax
from jax.experimental import pallas as pl
from jax.experimental.pallas import tpu as pltpu

F32 = jnp.float32
BF16 = jnp.bfloat16

D_MODEL = 2048
DEPTH = 4
CHUNK = 64
CONV_DIM = D_MODEL // 2
CONV_WIDTH = 31
HEAD_DIM = 64
N_Q_HEADS = (D_MODEL // 2) // HEAD_DIM
N_KV_HEADS = 4
GROUP = N_Q_HEADS // N_KV_HEADS
ATTN_DIM = N_Q_HEADS * HEAD_DIM
KV_DIM = N_KV_HEADS * HEAD_DIM
QK_DIM = ATTN_DIM + KV_DIM
WINDOW = 128
ROPE_DIM = HEAD_DIM // 4
ROPE_THETA = 500000.0
D_FF = ((8 * D_MODEL + 3 * 256 - 1) // (3 * 256)) * 256
EPS = 1e-6
PAST_LEN = 4096
LOG2E = 1.4426950408889634
Q_SCALE = (HEAD_DIM ** -0.5) * LOG2E

LANES = 128
SUBLANES = 8
MIB = 1024 * 1024
CONV_TILES = CONV_DIM // LANES

_NT = (((1,), (1,)), ((), ()))


def _params(sem, vmem_mib):
    return pltpu.CompilerParams(dimension_semantics=sem, vmem_limit_bytes=vmem_mib * MIB)


def _row_tile(m):
    return min(m, 1024)


def _parts(rows, n):
    step = rows // n
    return [(p * step, (p + 1) * step) for p in range(n)]


def _dot(a, b):
    return jnp.dot(a, b, preferred_element_type=F32)


def _rms_to_bf16(x, g):
    ms = jnp.mean(x * x, axis=-1, keepdims=True)
    return (x * lax.rsqrt(ms + EPS) * g).astype(BF16)


def _norm_kernel(x_ref, g_ref, hb_ref):
    hb_ref[...] = _rms_to_bf16(x_ref[...], g_ref[...])


def _norm(x, g, l):
    m = x.shape[0]
    tm = min(m, 512)
    return pl.pallas_call(
        _norm_kernel,
        grid=(m // tm,),
        in_specs=[
            pl.BlockSpec((tm, D_MODEL), lambda i: (i, 0)),
            pl.BlockSpec((None, 1, D_MODEL), lambda i: (l, 0, 0)),
        ],
        out_specs=pl.BlockSpec((tm, D_MODEL), lambda i: (i, 0)),
        out_shape=jax.ShapeDtypeStruct((m, D_MODEL), BF16),
        compiler_params=_params(("arbitrary",), 32),
        name="norm",
    )(x, g)


GLU_TN = 512
GLU_PARTS = 4


def _glu_kernel(hb_ref, wl_ref, wg_ref, u_ref, *, tm):
    parts = _parts(tm, GLU_PARTS)
    ab = []
    for r0, r1 in parts:
        h = hb_ref[r0:r1, :]
        ab.append((_dot(h, wl_ref[...]), _dot(h, wg_ref[...])))
    for (r0, r1), (a, b) in zip(parts, ab):
        u = a * jax.nn.sigmoid(b)
        for t in range(GLU_TN // LANES):
            u_ref[t, r0:r1, :] = u[:, t * LANES:(t + 1) * LANES]


def _glu(hb, wl, wg, l):
    m = hb.shape[0]
    tm, tn = _row_tile(m), GLU_TN
    return pl.pallas_call(
        functools.partial(_glu_kernel, tm=tm),
        grid=(m // tm, CONV_DIM // tn),
        in_specs=[
            pl.BlockSpec((tm, D_MODEL), lambda i, j: (i, 0)),
            pl.BlockSpec((None, None, D_MODEL, tn), lambda i, j: (l, j, 0, 0)),
            pl.BlockSpec((None, None, D_MODEL, tn), lambda i, j: (l, j, 0, 0)),
        ],
        out_specs=pl.BlockSpec((tn // LANES, tm, LANES), lambda i, j: (j, i, 0)),
        out_shape=jax.ShapeDtypeStruct((CONV_TILES, m, LANES), F32),
        compiler_params=_params(("arbitrary", "arbitrary"), 48),
        name="glu",
    )(hb, wl, wg)


QKV_PARTS = 2


def _split_bf16(a):
    hi = a.astype(BF16)
    lo = (a - hi.astype(F32)).astype(BF16)
    return hi, lo


def _qkv_kernel(hb_ref, w_ref, cos_ref, sa_ref, sb_ref, nw_ref, e_ref, et_ref, q_ref, k_ref, v_ref, *, tm):
    parts = _parts(tm, QKV_PARTS)
    zs = [_dot(hb_ref[r0:r1, :], w_ref[...]) for r0, r1 in parts]
    for (r0, r1), z in zip(parts, zs):
        v_ref[r0:r1, :] = z[:, QK_DIM:]
        qk = z[:, :QK_DIM]
        hi, lo = _split_bf16(qk * qk)
        e = e_ref[...]
        ms = _dot(hi, e) + _dot(lo, e)
        rhi, rlo = _split_bf16(lax.rsqrt(ms + EPS))
        et = et_ref[...]
        scale = _dot(rhi, et) + _dot(rlo, et)
        y = qk * scale * nw_ref[...]
        cos, sa, sb = cos_ref[r0:r1, :], sa_ref[r0:r1, :], sb_ref[r0:r1, :]
        for cg in range(QK_DIM // LANES):
            yc = y[:, cg * LANES:(cg + 1) * LANES]
            rot = yc * cos + pltpu.roll(yc, LANES - ROPE_DIM // 2, 1) * sa + pltpu.roll(yc, ROPE_DIM // 2, 1) * sb
            if cg < ATTN_DIM // LANES:
                q_ref[r0:r1, cg * LANES:(cg + 1) * LANES] = (rot * Q_SCALE).astype(BF16)
            else:
                c0 = cg * LANES - ATTN_DIM
                k_ref[r0:r1, c0:c0 + LANES] = rot


def _qkv(hb, w, tabs, nw, e, et, l):
    m = hb.shape[0]
    tm = min(m, 512)
    cos, sa, sb = tabs
    tab_spec = pl.BlockSpec((tm, LANES), lambda i: (i, 0))
    return pl.pallas_call(
        functools.partial(_qkv_kernel, tm=tm),
        grid=(m // tm,),
        in_specs=[
            pl.BlockSpec((tm, D_MODEL), lambda i: (i, 0)),
            pl.BlockSpec((None, D_MODEL, QK_DIM + KV_DIM), lambda i: (l, 0, 0)),
            tab_spec, tab_spec, tab_spec,
            pl.BlockSpec((None, 1, QK_DIM), lambda i: (l, 0, 0)),
            pl.BlockSpec((QK_DIM, LANES), lambda i: (0, 0)),
            pl.BlockSpec((LANES, QK_DIM), lambda i: (0, 0)),
        ],
        out_specs=[
            pl.BlockSpec((tm, ATTN_DIM), lambda i: (i, 0)),
            pl.BlockSpec((tm, KV_DIM), lambda i: (i, 0)),
            pl.BlockSpec((tm, KV_DIM), lambda i: (i, 0)),
        ],
        out_shape=[
            jax.ShapeDtypeStruct((m, ATTN_DIM), BF16),
            jax.ShapeDtypeStruct((m, KV_DIM), F32),
            jax.ShapeDtypeStruct((m, KV_DIM), F32),
        ],
        compiler_params=_params(("arbitrary",), 56),
        name="qkv",
    )(hb, w, cos, sa, sb, nw, e, et)


CONV_BLOCK = 512
CONV_ROWS = 64
LN_ROWS = 32
HALO = 32


def _conv_kernel(u_ref, halo_ref, w_ref, b_ref, lg_ref, lb_ref, c_ref, up_ref, y_ref):
    i = pl.program_id(0)

    @pl.when(i == 0)
    def _():
        up_ref[:, 0:HALO, :] = jnp.zeros((CONV_TILES, HALO, LANES), F32)

    @pl.when(i > 0)
    def _():
        up_ref[:, 0:HALO, :] = halo_ref[...]

    up_ref[:, HALO:, :] = u_ref[...]
    first = HALO - (CONV_WIDTH - 1)
    groups = CONV_ROWS // SUBLANES
    for ct in range(CONV_TILES):
        taps = [w_ref[ct, pl.ds(j, SUBLANES, stride=0), :] for j in range(CONV_WIDTH)]
        bias = b_ref[ct, pl.ds(0, SUBLANES, stride=0), :]
        for rc in range(CONV_BLOCK // CONV_ROWS):
            acc = jnp.broadcast_to(bias, (groups, SUBLANES, LANES))
            for j in range(CONV_WIDTH):
                s = rc * CONV_ROWS + first + j
                acc = acc + taps[j] * up_ref[ct, s:s + CONV_ROWS, :].reshape(groups, SUBLANES, LANES)
            y_ref[ct, rc * CONV_ROWS:(rc + 1) * CONV_ROWS, :] = acc.reshape(CONV_ROWS, LANES)

    lg, lb = lg_ref[...], lb_ref[...]
    for r in range(CONV_BLOCK // LN_ROWS):
        r0 = r * LN_ROWS
        y = y_ref[:, r0:r0 + LN_ROWS, :]
        mean = jnp.sum(jnp.sum(y, axis=0, keepdims=True), axis=2, keepdims=True) * (1.0 / CONV_DIM)
        yc = y - mean
        var = jnp.sum(jnp.sum(yc * yc, axis=0, keepdims=True), axis=2, keepdims=True) * (1.0 / CONV_DIM)
        z = yc * lax.rsqrt(var + EPS) * lg + lb
        z = (z * jax.nn.sigmoid(z)).astype(BF16)
        for ct in range(CONV_TILES):
            c_ref[r0:r0 + LN_ROWS, ct * LANES:(ct + 1) * LANES] = z[ct]


def _conv(u3, w, b, lg, lb, l):
    m = u3.shape[1]
    tr = CONV_BLOCK
    per = tr // HALO
    tiled = lambda a: a.reshape(DEPTH, CONV_TILES, 1, LANES)
    w_tiles = w.reshape(DEPTH, CONV_WIDTH, CONV_TILES, LANES).transpose(0, 2, 1, 3)
    tvec = pl.BlockSpec((None, CONV_TILES, 1, LANES), lambda i: (l, 0, 0, 0))
    return pl.pallas_call(
        _conv_kernel,
        grid=(m // tr,),
        in_specs=[
            pl.BlockSpec((CONV_TILES, tr, LANES), lambda i: (0, i, 0)),
            pl.BlockSpec((CONV_TILES, HALO, LANES), lambda i: (0, jnp.maximum(i * per - 1, 0), 0)),
            pl.BlockSpec((None, CONV_TILES, CONV_WIDTH, LANES), lambda i: (l, 0, 0, 0)),
            tvec, tvec, tvec,
        ],
        out_specs=pl.BlockSpec((tr, CONV_DIM), lambda i: (i, 0)),
        out_shape=jax.ShapeDtypeStruct((m, CONV_DIM), BF16),
        scratch_shapes=[
            pltpu.VMEM((CONV_TILES, HALO + tr, LANES), F32),
            pltpu.VMEM((CONV_TILES, tr, LANES), F32),
        ],
        compiler_params=_params(("arbitrary",), 32),
        name="conv",
    )(u3, u3, w_tiles, tiled(b), tiled(lg), tiled(lb))


CONV_S_BATCH = 8


def _ln_swish_bf16(acc, lg, lb):
    mean = jnp.mean(acc, axis=-1, keepdims=True)
    xc = acc - mean
    var = jnp.mean(xc * xc, axis=-1, keepdims=True)
    y = xc * lax.rsqrt(var + EPS) * lg + lb
    return (y * jax.nn.sigmoid(y)).astype(BF16)


def _conv_s_kernel(u_ref, hist_ref, w_ref, b_ref, lg_ref, lb_ref, c_ref, nh_ref, up_ref, *, t):
    nhist = CONV_WIDTH - 1
    for bb in range(CONV_S_BATCH):
        up_ref[0:nhist, :] = hist_ref[bb]
        up_ref[nhist:nhist + t, :] = u_ref[bb]
        acc = jnp.broadcast_to(b_ref[...], (t, CONV_DIM))
        for j in range(CONV_WIDTH):
            acc = acc + w_ref[j:j + 1, :] * up_ref[j:j + t, :]
        c_ref[bb] = _ln_swish_bf16(acc, lg_ref[...], lb_ref[...])
        nh_ref[bb] = up_ref[t:t + nhist, :]


def _conv_s(u, hist, w, b, lg, lb, l):
    nb, t, _ = u.shape
    nhist = CONV_WIDTH - 1
    vec = pl.BlockSpec((None, 1, CONV_DIM), lambda i: (l, 0, 0))
    return pl.pallas_call(
        functools.partial(_conv_s_kernel, t=t),
        grid=(nb // CONV_S_BATCH,),
        in_specs=[
            pl.BlockSpec((CONV_S_BATCH, t, CONV_DIM), lambda i: (i, 0, 0)),
            pl.BlockSpec((None, CONV_S_BATCH, nhist, CONV_DIM), lambda i: (l, i, 0, 0)),
            pl.BlockSpec((None, CONV_WIDTH, CONV_DIM), lambda i: (l, 0, 0)),
            vec, vec, vec,
        ],
        out_specs=[
            pl.BlockSpec((CONV_S_BATCH, t, CONV_DIM), lambda i: (i, 0, 0)),
            pl.BlockSpec((CONV_S_BATCH, nhist, CONV_DIM), lambda i: (i, 0, 0)),
        ],
        out_shape=[
            jax.ShapeDtypeStruct((nb, t, CONV_DIM), BF16),
            jax.ShapeDtypeStruct((nb, nhist, CONV_DIM), F32),
        ],
        scratch_shapes=[pltpu.VMEM((nhist + t + 2, CONV_DIM), F32)],
        compiler_params=_params(("arbitrary",), 32),
        name="conv_s",
    )(u, hist, w, b, lg, lb)


ATT_Q = 2 * CHUNK
ATT_K = 4 * CHUNK
ATT_BLOCK = 512
V_ROWS = HEAD_DIM + 16


def _sink_row(sink_ref, kh, cols_per_head):
    n = GROUP * cols_per_head
    lane_head = lax.broadcasted_iota(jnp.int32, (1, n), 1) // cols_per_head
    sink = jnp.full((1, n), sink_ref[kh * GROUP + GROUP - 1], F32)
    for g in range(GROUP - 2, -1, -1):
        sink = jnp.where(lane_head <= g, sink_ref[kh * GROUP + g], sink)
    return sink * LOG2E


def _attn_kernel(sink_ref, q_ref, k_ref, kh_ref, v_ref, vh_ref, o_ref, kbuf, vtbuf, otbuf):
    i = pl.program_id(0)
    kbuf[0:WINDOW, :] = kh_ref[...].astype(BF16)
    kbuf[WINDOW:, :] = k_ref[...].astype(BF16)
    vt_h = vh_ref[...].T.astype(BF16)
    vt_m = v_ref[...].T.astype(BF16)
    for kh in range(N_KV_HEADS):
        vtbuf[kh * V_ROWS:kh * V_ROWS + HEAD_DIM, 0:WINDOW] = vt_h[kh * HEAD_DIM:(kh + 1) * HEAD_DIM]
        vtbuf[kh * V_ROWS:kh * V_ROWS + HEAD_DIM, WINDOW:] = vt_m[kh * HEAD_DIM:(kh + 1) * HEAD_DIM]
        vtbuf[kh * V_ROWS + HEAD_DIM:(kh + 1) * V_ROWS, :] = jnp.ones((V_ROWS - HEAD_DIM, WINDOW + ATT_BLOCK), BF16)
    qt = q_ref[...].T

    n = GROUP * ATT_Q
    lane = lax.broadcasted_iota(jnp.int32, (CHUNK, n), 1)
    first_chunk_rows = (lane % ATT_Q) < CHUNK
    neg = -jnp.inf
    sinks = [_sink_row(sink_ref, kh, ATT_Q) for kh in range(N_KV_HEADS)]

    for t in range(ATT_BLOCK // ATT_Q):
        r0 = t * ATT_Q
        sts = []
        for kh in range(N_KV_HEADS):
            kk = kbuf[r0:r0 + ATT_K, kh * HEAD_DIM:(kh + 1) * HEAD_DIM]
            qtt = jnp.concatenate(
                [qt[(kh * GROUP + g) * HEAD_DIM:(kh * GROUP + g + 1) * HEAD_DIM, r0:r0 + ATT_Q] for g in range(GROUP)],
                axis=1)
            sts.append(_dot(kk, qtt))
        es, mxs = [], []
        for kh in range(N_KV_HEADS):
            st = sts[kh]
            c0 = jnp.where(first_chunk_rows, st[0:CHUNK], neg)
            c1 = st[CHUNK:2 * CHUNK]
            c3 = jnp.where(first_chunk_rows, neg, st[3 * CHUNK:])
            if t == 0:
                c0 = jnp.where(i > 0, c0, neg)
                c1 = jnp.where(i > 0, c1, neg)
            st = jnp.concatenate([c0, c1, st[2 * CHUNK:3 * CHUNK], c3], axis=0)
            mx = jnp.maximum(jnp.max(st, axis=0, keepdims=True), sinks[kh])
            es.append(jnp.exp2(st - mx).astype(BF16))
            mxs.append(mx)
        for kh in range(N_KV_HEADS):
            vt = vtbuf[kh * V_ROWS:(kh + 1) * V_ROWS, r0:r0 + ATT_K]
            ot = _dot(vt, es[kh])
            den = ot[HEAD_DIM:HEAD_DIM + 1] + jnp.exp2(sinks[kh] - mxs[kh])
            ot = ot[0:HEAD_DIM] / den
            for g in range(GROUP):
                h = kh * GROUP + g
                otbuf[h * HEAD_DIM:(h + 1) * HEAD_DIM, :] = ot[:, g * ATT_Q:(g + 1) * ATT_Q]
        o_ref[r0:r0 + ATT_Q, :] = otbuf[...].T.astype(BF16)


def _attn(q, k, v, sinks):
    m = q.shape[0]
    per = ATT_BLOCK // WINDOW
    main = pl.BlockSpec((ATT_BLOCK, KV_DIM), lambda i: (i, 0))
    halo = pl.BlockSpec((WINDOW, KV_DIM), lambda i: (jnp.maximum(i * per - 1, 0), 0))
    return pl.pallas_call(
        _attn_kernel,
        grid=(m // ATT_BLOCK,),
        in_specs=[
            pl.BlockSpec(memory_space=pltpu.SMEM),
            pl.BlockSpec((ATT_BLOCK, ATTN_DIM), lambda i: (i, 0)),
            main, halo, main, halo,
        ],
        out_specs=pl.BlockSpec((ATT_BLOCK, ATTN_DIM), lambda i: (i, 0)),
        out_shape=jax.ShapeDtypeStruct((m, ATTN_DIM), BF16),
        scratch_shapes=[
            pltpu.VMEM((WINDOW + ATT_BLOCK, KV_DIM), BF16),
            pltpu.VMEM((N_KV_HEADS * V_ROWS, WINDOW + ATT_BLOCK), BF16),
            pltpu.VMEM((ATTN_DIM, ATT_Q), F32),
        ],
        compiler_params=_params(("arbitrary",), 32),
        name="attn",
    )(sinks, q, k, k, v, v)


ATT_S_BATCH = 4


def _stack_heads(q_ref, r0, rows, kh):
    parts = [q_ref[r0:r0 + rows, (kh * GROUP + g) * HEAD_DIM:(kh * GROUP + g + 1) * HEAD_DIM] for g in range(GROUP)]
    return jnp.concatenate(parts, axis=0)


def _sink_column(sink_ref, kh, rows_per_head):
    n = GROUP * rows_per_head
    row = lax.broadcasted_iota(jnp.int32, (n, 1), 0)
    col = jnp.full((n, 1), sink_ref[kh * GROUP + GROUP - 1], F32)
    for g in range(GROUP - 2, -1, -1):
        col = jnp.where(row < (g + 1) * rows_per_head, sink_ref[kh * GROUP + g], col)
    return col * LOG2E


def _attn_s_kernel(sink_ref, q_ref, k_ref, v_ref, ck_ref, cv_ref, o_ref, nk_ref, nv_ref, *, t):
    keep = WINDOW - t
    sinks = [_sink_column(sink_ref, kh, t) for kh in range(N_KV_HEADS)]
    scores = []
    for bb in range(ATT_S_BATCH):
        r0 = bb * t
        nk_ref[bb, 0:keep, :] = ck_ref[bb, t:, :]
        nk_ref[bb, keep:, :] = k_ref[r0:r0 + t, :]
        nv_ref[bb, 0:keep, :] = cv_ref[bb, t:, :]
        nv_ref[bb, keep:, :] = v_ref[r0:r0 + t, :]
        for kh in range(N_KV_HEADS):
            cols = slice(kh * HEAD_DIM, (kh + 1) * HEAD_DIM)
            qs = _stack_heads(q_ref, r0, t, kh)
            s1 = lax.dot_general(qs, ck_ref[bb, :, cols].astype(BF16), _NT, preferred_element_type=F32)
            s2 = lax.dot_general(qs, k_ref[r0:r0 + t, cols].astype(BF16), _NT, preferred_element_type=F32)
            scores.append((s1, s2))
    probs = []
    for bb in range(ATT_S_BATCH):
        for kh in range(N_KV_HEADS):
            s1, s2 = scores[bb * N_KV_HEADS + kh]
            sink = sinks[kh]
            mx = jnp.maximum(jnp.maximum(jnp.max(s1, axis=-1, keepdims=True), jnp.max(s2, axis=-1, keepdims=True)), sink)
            e1 = jnp.exp2(s1 - mx)
            e2 = jnp.exp2(s2 - mx)
            den = jnp.sum(e1, axis=-1, keepdims=True) + jnp.sum(e2, axis=-1, keepdims=True) + jnp.exp2(sink - mx)
            probs.append((e1.astype(BF16), e2.astype(BF16), den))
    for bb in range(ATT_S_BATCH):
        r0 = bb * t
        for kh in range(N_KV_HEADS):
            cols = slice(kh * HEAD_DIM, (kh + 1) * HEAD_DIM)
            e1, e2, den = probs[bb * N_KV_HEADS + kh]
            o = _dot(e1, cv_ref[bb, :, cols].astype(BF16)) + _dot(e2, v_ref[r0:r0 + t, cols].astype(BF16))
            o = o / den
            o = jnp.concatenate([o[g * t:(g + 1) * t] for g in range(GROUP)], axis=1)
            o_ref[r0:r0 + t, kh * GROUP * HEAD_DIM:(kh + 1) * GROUP * HEAD_DIM] = o.astype(BF16)


def _attn_s(q, k, v, cache_k, cache_v, sinks, l, t):
    nb = q.shape[0] // t
    rows = ATT_S_BATCH * t
    new = pl.BlockSpec((rows, KV_DIM), lambda b: (b, 0))
    cache = pl.BlockSpec((None, ATT_S_BATCH, WINDOW, KV_DIM), lambda b: (l, b, 0, 0))
    out_cache = pl.BlockSpec((ATT_S_BATCH, WINDOW, KV_DIM), lambda b: (b, 0, 0))
    return pl.pallas_call(
        functools.partial(_attn_s_kernel, t=t),
        grid=(nb // ATT_S_BATCH,),
        in_specs=[
            pl.BlockSpec(memory_space=pltpu.SMEM),
            pl.BlockSpec((rows, ATTN_DIM), lambda b: (b, 0)),
            new, new, cache, cache,
        ],
        out_specs=[pl.BlockSpec((rows, ATTN_DIM), lambda b: (b, 0)), out_cache, out_cache],
        out_shape=[
            jax.ShapeDtypeStruct((nb * t, ATTN_DIM), BF16),
            jax.ShapeDtypeStruct((nb, WINDOW, KV_DIM), F32),
            jax.ShapeDtypeStruct((nb, WINDOW, KV_DIM), F32),
        ],
        compiler_params=_params(("arbitrary",), 32),
        name="attn_s",
    )(sinks, q, k, v, cache_k, cache_v)


MERGE_PARTS = 1
MERGE_TN = 512


def _merge_kernel(hb_ref, c_ref, o_ref, wgc_ref, wga_ref, wco_ref, wao_ref, mix_ref, *, tm):
    parts = _parts(tm, MERGE_PARTS)
    prods = []
    for r0, r1 in parts:
        h = hb_ref[r0:r1, :]
        prods.append((_dot(h, wgc_ref[...]), _dot(c_ref[r0:r1, :], wco_ref[...]),
                      _dot(h, wga_ref[...]), _dot(o_ref[r0:r1, :], wao_ref[...])))
    for (r0, r1), (gc, conv_out, ga, attn_out) in zip(parts, prods):
        mix = jax.nn.sigmoid(gc) * conv_out + jax.nn.sigmoid(ga) * attn_out
        mix_ref[r0:r1, :] = mix.astype(BF16)


def _merge(hb, c, o, wgc, wga, wco, wao, l):
    m = hb.shape[0]
    tm, tn = _row_tile(m), MERGE_TN
    wide = pl.BlockSpec((None, None, D_MODEL, tn), lambda i, j: (l, j, 0, 0))
    narrow = pl.BlockSpec((None, None, CONV_DIM, tn), lambda i, j: (l, j, 0, 0))
    rows = pl.BlockSpec((tm, CONV_DIM), lambda i, j: (i, 0))
    return pl.pallas_call(
        functools.partial(_merge_kernel, tm=tm),
        grid=(m // tm, D_MODEL // tn),
        in_specs=[pl.BlockSpec((tm, D_MODEL), lambda i, j: (i, 0)), rows, rows, wide, wide, narrow, narrow],
        out_specs=pl.BlockSpec((tm, tn), lambda i, j: (i, j)),
        out_shape=jax.ShapeDtypeStruct((m, D_MODEL), BF16),
        compiler_params=_params(("arbitrary", "arbitrary"), 48),
        name="merge",
    )(hb, c, o, wgc, wga, wco, wao)


OUT_PARTS = 2


def _outproj_kernel(x_ref, mix_ref, w_ref, g_ref, y_ref, hb_ref, *, tm):
    parts = _parts(tm, OUT_PARTS)
    prods = [_dot(mix_ref[r0:r1, :], w_ref[...]) for r0, r1 in parts]
    for (r0, r1), p in zip(parts, prods):
        y = x_ref[r0:r1, :] + p
        y_ref[r0:r1, :] = y
        hb_ref[r0:r1, :] = _rms_to_bf16(y, g_ref[...])


def _outproj(x, mix, w, g, l):
    m = x.shape[0]
    tm = min(m, 512)
    row = lambda i: (i, 0)
    return pl.pallas_call(
        functools.partial(_outproj_kernel, tm=tm),
        grid=(m // tm,),
        in_specs=[
            pl.BlockSpec((tm, D_MODEL), row),
            pl.BlockSpec((tm, D_MODEL), row),
            pl.BlockSpec((None, D_MODEL, D_MODEL), lambda i: (l, 0, 0)),
            pl.BlockSpec((None, 1, D_MODEL), lambda i: (l, 0, 0)),
        ],
        out_specs=[pl.BlockSpec((tm, D_MODEL), row), pl.BlockSpec((tm, D_MODEL), row)],
        out_shape=[jax.ShapeDtypeStruct((m, D_MODEL), F32), jax.ShapeDtypeStruct((m, D_MODEL), BF16)],
        compiler_params=_params(("arbitrary",), 56),
        name="outproj",
    )(x, mix, w, g)


FFN_TF = 512
FFN_PARTS = 1
RES_TILES = D_MODEL // FFN_TF


def _ffn_kernel(hb_ref, x_ref, g_ref, wg_ref, wu_ref, wd_ref, y_ref, *hbn_ref, tm):
    j = pl.program_id(1)

    @pl.when(j == 0)
    def _():
        y_ref[...] = jnp.zeros(y_ref.shape, F32)

    for jj in range(RES_TILES):
        @pl.when(j == jj)
        def _():
            y_ref[:, jj * FFN_TF:(jj + 1) * FFN_TF] += x_ref[...]

    parts = _parts(tm, FFN_PARTS)
    gu = []
    for r0, r1 in parts:
        h = hb_ref[r0:r1, :]
        gu.append((_dot(h, wg_ref[...]), _dot(h, wu_ref[...])))
    for (r0, r1), (gate, up) in zip(parts, gu):
        act = (gate * jax.nn.sigmoid(gate) * up).astype(BF16)
        y_ref[r0:r1, :] += _dot(act, wd_ref[...])

    if hbn_ref:
        @pl.when(j == pl.num_programs(1) - 1)
        def _():
            hbn_ref[0][...] = _rms_to_bf16(y_ref[...], g_ref[...])


def _ffn(hb, x, g_next, wg, wu, wd, l, emit_next):
    m = x.shape[0]
    tm, tf = _row_tile(m), FFN_TF
    row = lambda i, j: (i, 0)
    ln = min(l + 1, DEPTH - 1)
    out_specs = [pl.BlockSpec((tm, D_MODEL), row)]
    out_shape = [jax.ShapeDtypeStruct((m, D_MODEL), F32)]
    if emit_next:
        out_specs.append(pl.BlockSpec((tm, D_MODEL), row))
        out_shape.append(jax.ShapeDtypeStruct((m, D_MODEL), BF16))
    return pl.pallas_call(
        functools.partial(_ffn_kernel, tm=tm),
        grid=(m // tm, D_FF // tf),
        in_specs=[
            pl.BlockSpec((tm, D_MODEL), row),
            pl.BlockSpec((tm, tf), lambda i, j: (i, jnp.minimum(j, RES_TILES - 1))),
            pl.BlockSpec((None, 1, D_MODEL), lambda i, j: (ln, 0, 0)),
            pl.BlockSpec((None, None, D_MODEL, tf), lambda i, j: (l, j, 0, 0)),
            pl.BlockSpec((None, None, D_MODEL, tf), lambda i, j: (l, j, 0, 0)),
            pl.BlockSpec((None, tf, D_MODEL), lambda i, j: (l, j, 0)),
        ],
        out_specs=out_specs,
        out_shape=out_shape,
        compiler_params=_params(("arbitrary", "arbitrary"), 60),
        name="ffn",
    )(hb, x, g_next, wg, wu, wd)


def _rope_tables(pos):
    half = ROPE_DIM // 2
    inv_freq = ROPE_THETA ** (-jnp.arange(half, dtype=F32) * 2.0 / ROPE_DIM)
    ang = pos[:, None] * inv_freq[None, :]
    cos, sin = jnp.cos(ang), jnp.sin(ang)
    n = pos.shape[0]
    rest = HEAD_DIM - ROPE_DIM
    zeros_h = jnp.zeros((n, half), F32)
    cos_t = jnp.concatenate([cos, cos, jnp.ones((n, rest), F32)], axis=1)
    sa_t = jnp.concatenate([-sin, zeros_h, jnp.zeros((n, rest), F32)], axis=1)
    sb_t = jnp.concatenate([zeros_h, sin, jnp.zeros((n, rest), F32)], axis=1)
    reps = LANES // HEAD_DIM
    return tuple(jnp.tile(a, (1, reps)) for a in (cos_t, sa_t, sb_t))


def _head_pool_matrices():
    head = jnp.arange(QK_DIM, dtype=jnp.int32) // HEAD_DIM
    onehot = head[:, None] == jnp.arange(LANES, dtype=jnp.int32)[None, :]
    e = jnp.where(onehot, 1.0 / HEAD_DIM, 0.0).astype(BF16)
    et = jnp.where(onehot.T, 1.0, 0.0).astype(BF16)
    return e, et


def kernel(x_prompt, x_sample, state_conv, cache_k, cache_v, norm_mix_g, w_in, w_dw, b_dw, conv_ln_g,
           conv_ln_b, w_conv_out, q_norm_g, k_norm_g, sinks, w_attn_out, w_out, norm_ffn_g, w_gate_up, w_down):
    nb, t = x_sample.shape[0], x_sample.shape[1]
    seq = x_prompt.shape[1]

    def cut(w, a, b):
        return w[:, :, a:b].astype(BF16)

    def col_tiles(w, tn):
        d, k, n = w.shape
        return w.reshape(d, k, n // tn, tn).transpose(0, 2, 1, 3)

    c1 = CONV_DIM
    c2 = c1 + CONV_DIM
    c3 = c2 + QK_DIM + KV_DIM
    c4 = c3 + D_MODEL
    w_lin, w_gate = col_tiles(cut(w_in, 0, c1), GLU_TN), col_tiles(cut(w_in, c1, c2), GLU_TN)
    w_qkv = cut(w_in, c2, c3)
    w_gc, w_ga = col_tiles(cut(w_in, c3, c4), MERGE_TN), col_tiles(cut(w_in, c4, c4 + D_MODEL), MERGE_TN)
    w_ffg, w_ffu = col_tiles(cut(w_gate_up, 0, D_FF), FFN_TF), col_tiles(cut(w_gate_up, D_FF, 2 * D_FF), FFN_TF)
    w_co, w_ao = col_tiles(w_conv_out.astype(BF16), MERGE_TN), col_tiles(w_attn_out.astype(BF16), MERGE_TN)
    w_o, w_dn = w_out.astype(BF16), w_down.astype(BF16)

    g_mix = norm_mix_g.reshape(DEPTH, 1, D_MODEL)
    g_ffn = norm_ffn_g.reshape(DEPTH, 1, D_MODEL)
    b_conv = b_dw.reshape(DEPTH, 1, CONV_DIM)
    ln_g = conv_ln_g.reshape(DEPTH, 1, CONV_DIM)
    ln_b = conv_ln_b.reshape(DEPTH, 1, CONV_DIM)
    qk_w = jnp.concatenate([jnp.tile(q_norm_g, (1, N_Q_HEADS)), jnp.tile(k_norm_g, (1, N_KV_HEADS))], axis=1)
    qk_w = qk_w.reshape(DEPTH, 1, QK_DIM)
    e, et = _head_pool_matrices()

    tabs_p = _rope_tables(jnp.arange(seq, dtype=F32))
    tabs_s = _rope_tables(jnp.tile(jnp.arange(t, dtype=F32) + PAST_LEN, nb))
    ck = cache_k.reshape(DEPTH, nb, WINDOW, KV_DIM)
    cv = cache_v.reshape(DEPTH, nb, WINDOW, KV_DIM)

    def token_tail(x, hb, c, o, l):
        mix = _merge(hb, c, o, w_gc, w_ga, w_co, w_ao, l)
        x1, hb2 = _outproj(x, mix, w_o, g_ffn, l)
        out = _ffn(hb2, x1, g_mix, w_ffg, w_ffu, w_dn, l, l + 1 < DEPTH)
        return (out[0], out[1]) if l + 1 < DEPTH else (out[0], None)

    xp = x_prompt.reshape(seq, D_MODEL)
    xs = x_sample.reshape(nb * t, D_MODEL)
    hbp = _norm(xp, g_mix, 0)
    hbs = _norm(xs, g_mix, 0)
    hist_p, k_p, v_p, hist_s, k_s, v_s = [], [], [], [], [], []
    for l in range(DEPTH):
        u3 = _glu(hbp, w_lin, w_gate, l)
        q, k, v = _qkv(hbp, w_qkv, tabs_p, qk_w, e, et, l)
        c = _conv(u3, w_dw, b_conv, ln_g, ln_b, l)
        o = _attn(q, k, v, sinks[l])
        xp, hbp_next = token_tail(xp, hbp, c, o, l)
        hist_p.append(u3[:, seq - (CONV_WIDTH - 1):, :].transpose(1, 0, 2).reshape(1, CONV_WIDTH - 1, CONV_DIM))
        k_p.append(k[seq - WINDOW:].reshape(1, WINDOW, N_KV_HEADS, HEAD_DIM))
        v_p.append(v[seq - WINDOW:].reshape(1, WINDOW, N_KV_HEADS, HEAD_DIM))
        hbp = hbp_next

        u3 = _glu(hbs, w_lin, w_gate, l)
        q, k, v = _qkv(hbs, w_qkv, tabs_s, qk_w, e, et, l)
        u = u3.transpose(1, 0, 2).reshape(nb, t, CONV_DIM)
        c, nh = _conv_s(u, state_conv, w_dw, b_conv, ln_g, ln_b, l)
        o, nk, nv = _attn_s(q, k, v, ck, cv, sinks[l], l, t)
        xs, hbs_next = token_tail(xs, hbs, c.reshape(nb * t, CONV_DIM), o, l)
        hist_s.append(nh)
        k_s.append(nk.reshape(nb, WINDOW, N_KV_HEADS, HEAD_DIM))
        v_s.append(nv.reshape(nb, WINDOW, N_KV_HEADS, HEAD_DIM))
        hbs = hbs_next

    return (xp.reshape(x_prompt.shape), xs.reshape(x_sample.shape),
            jnp.stack(hist_p), jnp.stack(k_p), jnp.stack(v_p),
            jnp.stack(hist_s), jnp.stack(k_s), jnp.stack(v_s))
```

```python
import functools

import jax
import jax.numpy as jnp
from jax import lax
from jax.experimental import pallas as pl
from jax.experimental.pallas import tpu as pltpu

F32 = jnp.float32
BF16 = jnp.bfloat16

D_MODEL = 2048
DEPTH = 4
CHUNK = 64
CONV_DIM = D_MODEL // 2
CONV_WIDTH = 31
HEAD_DIM = 64
N_Q_HEADS = (D_MODEL // 2) // HEAD_DIM
N_KV_HEADS = 4
GROUP = N_Q_HEADS // N_KV_HEADS
ATTN_DIM = N_Q_HEADS * HEAD_DIM
KV_DIM = N_KV_HEADS * HEAD_DIM
QK_DIM = ATTN_DIM + KV_DIM
WINDOW = 128
ROPE_DIM = HEAD_DIM // 4
ROPE_THETA = 500000.0
D_FF = ((8 * D_MODEL + 3 * 256 - 1) // (3 * 256)) * 256
EPS = 1e-6
PAST_LEN = 4096
LOG2E = 1.4426950408889634
Q_SCALE = (HEAD_DIM ** -0.5) * LOG2E

LANES = 128
SUBLANES = 8
MIB = 1024 * 1024
CONV_TILES = CONV_DIM // LANES

_NT = (((1,), (1,)), ((), ()))


def _params(sem, vmem_mib):
    return pltpu.CompilerParams(dimension_semantics=sem, vmem_limit_bytes=vmem_mib * MIB)


def _row_tile(m):
    return min(m, 1024)


def _parts(rows, n):
    step = rows // n
    return [(p * step, (p + 1) * step) for p in range(n)]


def _dot(a, b):
    return jnp.dot(a, b, preferred_element_type=F32)


def _rms_to_bf16(x, g):
    ms = jnp.mean(x * x, axis=-1, keepdims=True)
    return (x * lax.rsqrt(ms + EPS) * g).astype(BF16)


def _norm_kernel(x_ref, g_ref, hb_ref):
    hb_ref[...] = _rms_to_bf16(x_ref[...], g_ref[...])


def _norm(x, g, l):
    m = x.shape[0]
    tm = min(m, 512)
    return pl.pallas_call(
        _norm_kernel,
        grid=(m // tm,),
        in_specs=[
            pl.BlockSpec((tm, D_MODEL), lambda i: (i, 0)),
            pl.BlockSpec((None, 1, D_MODEL), lambda i: (l, 0, 0)),
        ],
        out_specs=pl.BlockSpec((tm, D_MODEL), lambda i: (i, 0)),
        out_shape=jax.ShapeDtypeStruct((m, D_MODEL), BF16),
        compiler_params=_params(("arbitrary",), 32),
        name="norm",
    )(x, g)


GLU_TN = 1024
GLU_PARTS = 4


def _glu_kernel(hb_ref, wl_ref, wg_ref, u_ref, *, tm):
    parts = _parts(tm, GLU_PARTS)
    ab = []
    for r0, r1 in parts:
        h = hb_ref[r0:r1, :]
        ab.append((_dot(h, wl_ref[...]), _dot(h, wg_ref[...])))
    for (r0, r1), (a, b) in zip(parts, ab):
        u = a * jax.nn.sigmoid(b)
        for t in range(GLU_TN // LANES):
            u_ref[t, r0:r1, :] = u[:, t * LANES:(t + 1) * LANES]


def _glu(hb, wl, wg, l):
    m = hb.shape[0]
    tm, tn = _row_tile(m), GLU_TN
    return pl.pallas_call(
        functools.partial(_glu_kernel, tm=tm),
        grid=(m // tm, CONV_DIM // tn),
        in_specs=[
            pl.BlockSpec((tm, D_MODEL), lambda i, j: (i, 0)),
            pl.BlockSpec((None, D_MODEL, tn), lambda i, j: (l, 0, j)),
            pl.BlockSpec((None, D_MODEL, tn), lambda i, j: (l, 0, j)),
        ],
        out_specs=pl.BlockSpec((tn // LANES, tm, LANES), lambda i, j: (j, i, 0)),
        out_shape=jax.ShapeDtypeStruct((CONV_TILES, m, LANES), F32),
        compiler_params=_params(("arbitrary", "arbitrary"), 48),
        name="glu",
    )(hb, wl, wg)


QKV_PART_ROWS = 256


def _split_bf16(a):
    hi = a.astype(BF16)
    lo = (a - hi.astype(F32)).astype(BF16)
    return hi, lo


def _qkv_kernel(hb_ref, w_ref, cos_ref, sa_ref, sb_ref, nw_ref, e_ref, et_ref, q_ref, k_ref, v_ref, *, tm):
    parts = _parts(tm, tm // QKV_PART_ROWS)
    zs = [_dot(hb_ref[r0:r1, :], w_ref[...]) for r0, r1 in parts]
    for (r0, r1), z in zip(parts, zs):
        v_ref[r0:r1, :] = z[:, QK_DIM:]
        qk = z[:, :QK_DIM]
        hi, lo = _split_bf16(qk * qk)
        e = e_ref[...]
        ms = _dot(hi, e) + _dot(lo, e)
        rhi, rlo = _split_bf16(lax.rsqrt(ms + EPS))
        et = et_ref[...]
        scale = _dot(rhi, et) + _dot(rlo, et)
        y = qk * scale * nw_ref[...]
        cos, sa, sb = cos_ref[r0:r1, :], sa_ref[r0:r1, :], sb_ref[r0:r1, :]
        for cg in range(QK_DIM // LANES):
            yc = y[:, cg * LANES:(cg + 1) * LANES]
            rot = yc * cos + pltpu.roll(yc, LANES - ROPE_DIM // 2, 1) * sa + pltpu.roll(yc, ROPE_DIM // 2, 1) * sb
            if cg < ATTN_DIM // LANES:
                q_ref[r0:r1, cg * LANES:(cg + 1) * LANES] = (rot * Q_SCALE).astype(BF16)
            else:
                c0 = cg * LANES - ATTN_DIM
                k_ref[r0:r1, c0:c0 + LANES] = rot


def _qkv(hb, w, tabs, nw, e, et, l):
    m = hb.shape[0]
    tm = _row_tile(m)
    cos, sa, sb = tabs
    tab_spec = pl.BlockSpec((tm, LANES), lambda i: (i, 0))
    return pl.pallas_call(
        functools.partial(_qkv_kernel, tm=tm),
        grid=(m // tm,),
        in_specs=[
            pl.BlockSpec((tm, D_MODEL), lambda i: (i, 0)),
            pl.BlockSpec((None, D_MODEL, QK_DIM + KV_DIM), lambda i: (l, 0, 0)),
            tab_spec, tab_spec, tab_spec,
            pl.BlockSpec((None, 1, QK_DIM), lambda i: (l, 0, 0)),
            pl.BlockSpec((QK_DIM, LANES), lambda i: (0, 0)),
            pl.BlockSpec((LANES, QK_DIM), lambda i: (0, 0)),
        ],
        out_specs=[
            pl.BlockSpec((tm, ATTN_DIM), lambda i: (i, 0)),
            pl.BlockSpec((tm, KV_DIM), lambda i: (i, 0)),
            pl.BlockSpec((tm, KV_DIM), lambda i: (i, 0)),
        ],
        out_shape=[
            jax.ShapeDtypeStruct((m, ATTN_DIM), BF16),
            jax.ShapeDtypeStruct((m, KV_DIM), F32),
            jax.ShapeDtypeStruct((m, KV_DIM), F32),
        ],
        compiler_params=_params(("arbitrary",), 56),
        name="qkv",
    )(hb, w, cos, sa, sb, nw, e, et)


CONV_BLOCK = 512
CONV_ROWS = 64
LN_ROWS = 32
HALO = 32


def _conv_kernel(u_ref, halo_ref, w_ref, b_ref, lg_ref, lb_ref, c_ref, up_ref, y_ref):
    i = pl.program_id(0)

    @pl.when(i == 0)
    def _():
        up_ref[:, 0:HALO, :] = jnp.zeros((CONV_TILES, HALO, LANES), F32)

    @pl.when(i > 0)
    def _():
        up_ref[:, 0:HALO, :] = halo_ref[...]

    up_ref[:, HALO:, :] = u_ref[...]
    first = HALO - (CONV_WIDTH - 1)
    groups = CONV_ROWS // SUBLANES
    for ct in range(CONV_TILES):
        taps = [w_ref[ct, pl.ds(j, SUBLANES, stride=0), :] for j in range(CONV_WIDTH)]
        bias = b_ref[ct, pl.ds(0, SUBLANES, stride=0), :]
        for rc in range(CONV_BLOCK // CONV_ROWS):
            acc = jnp.broadcast_to(bias, (groups, SUBLANES, LANES))
            for j in range(CONV_WIDTH):
                s = rc * CONV_ROWS + first + j
                acc = acc + taps[j] * up_ref[ct, s:s + CONV_ROWS, :].reshape(groups, SUBLANES, LANES)
            y_ref[ct, rc * CONV_ROWS:(rc + 1) * CONV_ROWS, :] = acc.reshape(CONV_ROWS, LANES)

    lg, lb = lg_ref[...], lb_ref[...]
    for r in range(CONV_BLOCK // LN_ROWS):
        r0 = r * LN_ROWS
        y = y_ref[:, r0:r0 + LN_ROWS, :]
        mean = jnp.sum(jnp.sum(y, axis=0, keepdims=True), axis=2, keepdims=True) * (1.0 / CONV_DIM)
        yc = y - mean
        var = jnp.sum(jnp.sum(yc * yc, axis=0, keepdims=True), axis=2, keepdims=True) * (1.0 / CONV_DIM)
        z = yc * lax.rsqrt(var + EPS) * lg + lb
        z = (z * jax.nn.sigmoid(z)).astype(BF16)
        for ct in range(CONV_TILES):
            c_ref[r0:r0 + LN_ROWS, ct * LANES:(ct + 1) * LANES] = z[ct]


def _conv(u3, w, b, lg, lb, l):
    m = u3.shape[1]
    tr = CONV_BLOCK
    per = tr // HALO
    tiled = lambda a: a.reshape(DEPTH, CONV_TILES, 1, LANES)
    w_tiles = w.reshape(DEPTH, CONV_WIDTH, CONV_TILES, LANES).transpose(0, 2, 1, 3)
    tvec = pl.BlockSpec((None, CONV_TILES, 1, LANES), lambda i: (l, 0, 0, 0))
    return pl.pallas_call(
        _conv_kernel,
        grid=(m // tr,),
        in_specs=[
            pl.BlockSpec((CONV_TILES, tr, LANES), lambda i: (0, i, 0)),
            pl.BlockSpec((CONV_TILES, HALO, LANES), lambda i: (0, jnp.maximum(i * per - 1, 0), 0)),
            pl.BlockSpec((None, CONV_TILES, CONV_WIDTH, LANES), lambda i: (l, 0, 0, 0)),
            tvec, tvec, tvec,
        ],
        out_specs=pl.BlockSpec((tr, CONV_DIM), lambda i: (i, 0)),
        out_shape=jax.ShapeDtypeStruct((m, CONV_DIM), BF16),
        scratch_shapes=[
            pltpu.VMEM((CONV_TILES, HALO + tr, LANES), F32),
            pltpu.VMEM((CONV_TILES, tr, LANES), F32),
        ],
        compiler_params=_params(("arbitrary",), 32),
        name="conv",
    )(u3, u3, w_tiles, tiled(b), tiled(lg), tiled(lb))


CONV_S_BATCH = 8


def _ln_swish_bf16(acc, lg, lb):
    mean = jnp.mean(acc, axis=-1, keepdims=True)
    xc = acc - mean
    var = jnp.mean(xc * xc, axis=-1, keepdims=True)
    y = xc * lax.rsqrt(var + EPS) * lg + lb
    return (y * jax.nn.sigmoid(y)).astype(BF16)


def _conv_s_kernel(u_ref, hist_ref, w_ref, b_ref, lg_ref, lb_ref, c_ref, nh_ref, up_ref, *, t):
    nhist = CONV_WIDTH - 1
    for bb in range(CONV_S_BATCH):
        up_ref[0:nhist, :] = hist_ref[bb]
        up_ref[nhist:nhist + t, :] = u_ref[bb]
        acc = jnp.broadcast_to(b_ref[...], (t, CONV_DIM))
        for j in range(CONV_WIDTH):
            acc = acc + w_ref[j:j + 1, :] * up_ref[j:j + t, :]
        c_ref[bb] = _ln_swish_bf16(acc, lg_ref[...], lb_ref[...])
        nh_ref[bb] = up_ref[t:t + nhist, :]


def _conv_s(u, hist, w, b, lg, lb, l):
    nb, t, _ = u.shape
    nhist = CONV_WIDTH - 1
    vec = pl.BlockSpec((None, 1, CONV_DIM), lambda i: (l, 0, 0))
    return pl.pallas_call(
        functools.partial(_conv_s_kernel, t=t),
        grid=(nb // CONV_S_BATCH,),
        in_specs=[
            pl.BlockSpec((CONV_S_BATCH, t, CONV_DIM), lambda i: (i, 0, 0)),
            pl.BlockSpec((None, CONV_S_BATCH, nhist, CONV_DIM), lambda i: (l, i, 0, 0)),
            pl.BlockSpec((None, CONV_WIDTH, CONV_DIM), lambda i: (l, 0, 0)),
            vec, vec, vec,
        ],
        out_specs=[
            pl.BlockSpec((CONV_S_BATCH, t, CONV_DIM), lambda i: (i, 0, 0)),
            pl.BlockSpec((CONV_S_BATCH, nhist, CONV_DIM), lambda i: (i, 0, 0)),
        ],
        out_shape=[
            jax.ShapeDtypeStruct((nb, t, CONV_DIM), BF16),
            jax.ShapeDtypeStruct((nb, nhist, CONV_DIM), F32),
        ],
        scratch_shapes=[pltpu.VMEM((nhist + t + 2, CONV_DIM), F32)],
        compiler_params=_params(("arbitrary",), 32),
        name="conv_s",
    )(u, hist, w, b, lg, lb)


ATT_Q = 2 * CHUNK
ATT_K = 4 * CHUNK
ATT_BLOCK = 1024
V_ROWS = HEAD_DIM + 16


def _sink_row(sink_ref, kh, cols_per_head):
    n = GROUP * cols_per_head
    lane_head = lax.broadcasted_iota(jnp.int32, (1, n), 1) // cols_per_head
    sink = jnp.full((1, n), sink_ref[kh * GROUP + GROUP - 1], F32)
    for g in range(GROUP - 2, -1, -1):
        sink = jnp.where(lane_head <= g, sink_ref[kh * GROUP + g], sink)
    return sink * LOG2E


def _attn_kernel(sink_ref, q_ref, k_ref, kh_ref, v_ref, vh_ref, o_ref, kbuf, vtbuf, otbuf):
    i = pl.program_id(0)
    kbuf[0:WINDOW, :] = kh_ref[...].astype(BF16)
    kbuf[WINDOW:, :] = k_ref[...].astype(BF16)
    vt_h = vh_ref[...].T.astype(BF16)
    vt_m = v_ref[...].T.astype(BF16)
    for kh in range(N_KV_HEADS):
        vtbuf[kh * V_ROWS:kh * V_ROWS + HEAD_DIM, 0:WINDOW] = vt_h[kh * HEAD_DIM:(kh + 1) * HEAD_DIM]
        vtbuf[kh * V_ROWS:kh * V_ROWS + HEAD_DIM, WINDOW:] = vt_m[kh * HEAD_DIM:(kh + 1) * HEAD_DIM]
        vtbuf[kh * V_ROWS + HEAD_DIM:(kh + 1) * V_ROWS, :] = jnp.ones((V_ROWS - HEAD_DIM, WINDOW + ATT_BLOCK), BF16)
    qt = q_ref[...].T

    n = GROUP * ATT_Q
    lane = lax.broadcasted_iota(jnp.int32, (CHUNK, n), 1)
    first_chunk_rows = (lane % ATT_Q) < CHUNK
    neg = -jnp.inf
    sinks = [_sink_row(sink_ref, kh, ATT_Q) for kh in range(N_KV_HEADS)]

    for t in range(ATT_BLOCK // ATT_Q):
        r0 = t * ATT_Q
        sts = []
        for kh in range(N_KV_HEADS):
            kk = kbuf[r0:r0 + ATT_K, kh * HEAD_DIM:(kh + 1) * HEAD_DIM]
            qtt = jnp.concatenate(
                [qt[(kh * GROUP + g) * HEAD_DIM:(kh * GROUP + g + 1) * HEAD_DIM, r0:r0 + ATT_Q] for g in range(GROUP)],
                axis=1)
            sts.append(_dot(kk, qtt))
        es, mxs = [], []
        for kh in range(N_KV_HEADS):
            st = sts[kh]
            c0 = jnp.where(first_chunk_rows, st[0:CHUNK], neg)
            c1 = st[CHUNK:2 * CHUNK]
            c3 = jnp.where(first_chunk_rows, neg, st[3 * CHUNK:])
            if t == 0:
                c0 = jnp.where(i > 0, c0, neg)
                c1 = jnp.where(i > 0, c1, neg)
            st = jnp.concatenate([c0, c1, st[2 * CHUNK:3 * CHUNK], c3], axis=0)
            mx = jnp.maximum(jnp.max(st, axis=0, keepdims=True), sinks[kh])
            es.append(jnp.exp2(st - mx).astype(BF16))
            mxs.append(mx)
        for kh in range(N_KV_HEADS):
            vt = vtbuf[kh * V_ROWS:(kh + 1) * V_ROWS, r0:r0 + ATT_K]
            ot = _dot(vt, es[kh])
            den = ot[HEAD_DIM:HEAD_DIM + 1] + jnp.exp2(sinks[kh] - mxs[kh])
            ot = ot[0:HEAD_DIM] / den
            for g in range(GROUP):
                h = kh * GROUP + g
                otbuf[h * HEAD_DIM:(h + 1) * HEAD_DIM, :] = ot[:, g * ATT_Q:(g + 1) * ATT_Q]
        o_ref[r0:r0 + ATT_Q, :] = otbuf[...].T.astype(BF16)


def _attn(q, k, v, sinks):
    m = q.shape[0]
    per = ATT_BLOCK // WINDOW
    main = pl.BlockSpec((ATT_BLOCK, KV_DIM), lambda i: (i, 0))
    halo = pl.BlockSpec((WINDOW, KV_DIM), lambda i: (jnp.maximum(i * per - 1, 0), 0))
    return pl.pallas_call(
        _attn_kernel,
        grid=(m // ATT_BLOCK,),
        in_specs=[
            pl.BlockSpec(memory_space=pltpu.SMEM),
            pl.BlockSpec((ATT_BLOCK, ATTN_DIM), lambda i: (i, 0)),
            main, halo, main, halo,
        ],
        out_specs=pl.BlockSpec((ATT_BLOCK, ATTN_DIM), lambda i: (i, 0)),
        out_shape=jax.ShapeDtypeStruct((m, ATTN_DIM), BF16),
        scratch_shapes=[
            pltpu.VMEM((WINDOW + ATT_BLOCK, KV_DIM), BF16),
            pltpu.VMEM((N_KV_HEADS * V_ROWS, WINDOW + ATT_BLOCK), BF16),
            pltpu.VMEM((ATTN_DIM, ATT_Q), F32),
        ],
        compiler_params=_params(("arbitrary",), 32),
        name="attn",
    )(sinks, q, k, k, v, v)


ATT_S_BATCH = 4


def _stack_heads(q_ref, r0, rows, kh):
    parts = [q_ref[r0:r0 + rows, (kh * GROUP + g) * HEAD_DIM:(kh * GROUP + g + 1) * HEAD_DIM] for g in range(GROUP)]
    return jnp.concatenate(parts, axis=0)


def _sink_column(sink_ref, kh, rows_per_head):
    n = GROUP * rows_per_head
    row = lax.broadcasted_iota(jnp.int32, (n, 1), 0)
    col = jnp.full((n, 1), sink_ref[kh * GROUP + GROUP - 1], F32)
    for g in range(GROUP - 2, -1, -1):
        col = jnp.where(row < (g + 1) * rows_per_head, sink_ref[kh * GROUP + g], col)
    return col * LOG2E


def _attn_s_kernel(sink_ref, q_ref, k_ref, v_ref, ck_ref, cv_ref, o_ref, nk_ref, nv_ref, *, t):
    keep = WINDOW - t
    sinks = [_sink_column(sink_ref, kh, t) for kh in range(N_KV_HEADS)]
    scores = []
    for bb in range(ATT_S_BATCH):
        r0 = bb * t
        nk_ref[bb, 0:keep, :] = ck_ref[bb, t:, :]
        nk_ref[bb, keep:, :] = k_ref[r0:r0 + t, :]
        nv_ref[bb, 0:keep, :] = cv_ref[bb, t:, :]
        nv_ref[bb, keep:, :] = v_ref[r0:r0 + t, :]
        for kh in range(N_KV_HEADS):
            cols = slice(kh * HEAD_DIM, (kh + 1) * HEAD_DIM)
            qs = _stack_heads(q_ref, r0, t, kh)
            s1 = lax.dot_general(qs, ck_ref[bb, :, cols].astype(BF16), _NT, preferred_element_type=F32)
            s2 = lax.dot_general(qs, k_ref[r0:r0 + t, cols].astype(BF16), _NT, preferred_element_type=F32)
            scores.append((s1, s2))
    probs = []
    for bb in range(ATT_S_BATCH):
        for kh in range(N_KV_HEADS):
            s1, s2 = scores[bb * N_KV_HEADS + kh]
            sink = sinks[kh]
            mx = jnp.maximum(jnp.maximum(jnp.max(s1, axis=-1, keepdims=True), jnp.max(s2, axis=-1, keepdims=True)), sink)
            e1 = jnp.exp2(s1 - mx)
            e2 = jnp.exp2(s2 - mx)
            den = jnp.sum(e1, axis=-1, keepdims=True) + jnp.sum(e2, axis=-1, keepdims=True) + jnp.exp2(sink - mx)
            probs.append((e1.astype(BF16), e2.astype(BF16), den))
    for bb in range(ATT_S_BATCH):
        r0 = bb * t
        for kh in range(N_KV_HEADS):
            cols = slice(kh * HEAD_DIM, (kh + 1) * HEAD_DIM)
            e1, e2, den = probs[bb * N_KV_HEADS + kh]
            o = _dot(e1, cv_ref[bb, :, cols].astype(BF16)) + _dot(e2, v_ref[r0:r0 + t, cols].astype(BF16))
            o = o / den
            o = jnp.concatenate([o[g * t:(g + 1) * t] for g in range(GROUP)], axis=1)
            o_ref[r0:r0 + t, kh * GROUP * HEAD_DIM:(kh + 1) * GROUP * HEAD_DIM] = o.astype(BF16)


def _attn_s(q, k, v, cache_k, cache_v, sinks, l, t):
    nb = q.shape[0] // t
    rows = ATT_S_BATCH * t
    new = pl.BlockSpec((rows, KV_DIM), lambda b: (b, 0))
    cache = pl.BlockSpec((None, ATT_S_BATCH, WINDOW, KV_DIM), lambda b: (l, b, 0, 0))
    out_cache = pl.BlockSpec((ATT_S_BATCH, WINDOW, KV_DIM), lambda b: (b, 0, 0))
    return pl.pallas_call(
        functools.partial(_attn_s_kernel, t=t),
        grid=(nb // ATT_S_BATCH,),
        in_specs=[
            pl.BlockSpec(memory_space=pltpu.SMEM),
            pl.BlockSpec((rows, ATTN_DIM), lambda b: (b, 0)),
            new, new, cache, cache,
        ],
        out_specs=[pl.BlockSpec((rows, ATTN_DIM), lambda b: (b, 0)), out_cache, out_cache],
        out_shape=[
            jax.ShapeDtypeStruct((nb * t, ATTN_DIM), BF16),
            jax.ShapeDtypeStruct((nb, WINDOW, KV_DIM), F32),
            jax.ShapeDtypeStruct((nb, WINDOW, KV_DIM), F32),
        ],
        compiler_params=_params(("arbitrary",), 32),
        name="attn_s",
    )(sinks, q, k, v, cache_k, cache_v)


MERGE_PARTS = 1
MERGE_TN = 512


def _merge_kernel(hb_ref, c_ref, o_ref, wgc_ref, wga_ref, wco_ref, wao_ref, mix_ref, *, tm):
    parts = _parts(tm, MERGE_PARTS)
    prods = []
    for r0, r1 in parts:
        h = hb_ref[r0:r1, :]
        prods.append((_dot(h, wgc_ref[...]), _dot(c_ref[r0:r1, :], wco_ref[...]),
                      _dot(h, wga_ref[...]), _dot(o_ref[r0:r1, :], wao_ref[...])))
    for (r0, r1), (gc, conv_out, ga, attn_out) in zip(parts, prods):
        mix = jax.nn.sigmoid(gc) * conv_out + jax.nn.sigmoid(ga) * attn_out
        mix_ref[r0:r1, :] = mix.astype(BF16)


def _merge(hb, c, o, wgc, wga, wco, wao, l):
    m = hb.shape[0]
    tm, tn = _row_tile(m), MERGE_TN
    wide = pl.BlockSpec((None, D_MODEL, tn), lambda i, j: (l, 0, j))
    narrow = pl.BlockSpec((None, CONV_DIM, tn), lambda i, j: (l, 0, j))
    rows = pl.BlockSpec((tm, CONV_DIM), lambda i, j: (i, 0))
    return pl.pallas_call(
        functools.partial(_merge_kernel, tm=tm),
        grid=(m // tm, D_MODEL // tn),
        in_specs=[pl.BlockSpec((tm, D_MODEL), lambda i, j: (i, 0)), rows, rows, wide, wide, narrow, narrow],
        out_specs=pl.BlockSpec((tm, tn), lambda i, j: (i, j)),
        out_shape=jax.ShapeDtypeStruct((m, D_MODEL), BF16),
        compiler_params=_params(("arbitrary", "arbitrary"), 48),
        name="merge",
    )(hb, c, o, wgc, wga, wco, wao)


OUT_PARTS = 2


def _outproj_kernel(x_ref, mix_ref, w_ref, g_ref, y_ref, hb_ref, *, tm):
    parts = _parts(tm, OUT_PARTS)
    prods = [_dot(mix_ref[r0:r1, :], w_ref[...]) for r0, r1 in parts]
    for (r0, r1), p in zip(parts, prods):
        y = x_ref[r0:r1, :] + p
        y_ref[r0:r1, :] = y
        hb_ref[r0:r1, :] = _rms_to_bf16(y, g_ref[...])


def _outproj(x, mix, w, g, l):
    m = x.shape[0]
    tm = min(m, 512)
    row = lambda i: (i, 0)
    return pl.pallas_call(
        functools.partial(_outproj_kernel, tm=tm),
        grid=(m // tm,),
        in_specs=[
            pl.BlockSpec((tm, D_MODEL), row),
            pl.BlockSpec((tm, D_MODEL), row),
            pl.BlockSpec((None, D_MODEL, D_MODEL), lambda i: (l, 0, 0)),
            pl.BlockSpec((None, 1, D_MODEL), lambda i: (l, 0, 0)),
        ],
        out_specs=[pl.BlockSpec((tm, D_MODEL), row), pl.BlockSpec((tm, D_MODEL), row)],
        out_shape=[jax.ShapeDtypeStruct((m, D_MODEL), F32), jax.ShapeDtypeStruct((m, D_MODEL), BF16)],
        compiler_params=_params(("arbitrary",), 56),
        name="outproj",
    )(x, mix, w, g)


FFN_TF = 512
RES_TILES = D_MODEL // FFN_TF


def _ffn_kernel(hb_ref, x_ref, g_ref, wg_ref, wu_ref, wd_ref, y_ref, *hbn_ref):
    j = pl.program_id(1)

    @pl.when(j == 0)
    def _():
        y_ref[...] = jnp.zeros(y_ref.shape, F32)

    for jj in range(RES_TILES):
        @pl.when(j == jj)
        def _():
            y_ref[:, jj * FFN_TF:(jj + 1) * FFN_TF] += x_ref[...]

    h = hb_ref[...]
    gate = _dot(h, wg_ref[...])
    up = _dot(h, wu_ref[...])
    act = (gate * jax.nn.sigmoid(gate) * up).astype(BF16)
    y_ref[...] += _dot(act, wd_ref[...])

    if hbn_ref:
        @pl.when(j == pl.num_programs(1) - 1)
        def _():
            hbn_ref[0][...] = _rms_to_bf16(y_ref[...], g_ref[...])


def _ffn(hb, x, g_next, wg, wu, wd, l, emit_next):
    m = x.shape[0]
    tm, tf = _row_tile(m), FFN_TF
    row = lambda i, j: (i, 0)
    ln = min(l + 1, DEPTH - 1)
    out_specs = [pl.BlockSpec((tm, D_MODEL), row)]
    out_shape = [jax.ShapeDtypeStruct((m, D_MODEL), F32)]
    if emit_next:
        out_specs.append(pl.BlockSpec((tm, D_MODEL), row))
        out_shape.append(jax.ShapeDtypeStruct((m, D_MODEL), BF16))
    return pl.pallas_call(
        _ffn_kernel,
        grid=(m // tm, D_FF // tf),
        in_specs=[
            pl.BlockSpec((tm, D_MODEL), row),
            pl.BlockSpec((tm, tf), lambda i, j: (i, jnp.minimum(j, RES_TILES - 1))),
            pl.BlockSpec((None, 1, D_MODEL), lambda i, j: (ln, 0, 0)),
            pl.BlockSpec((None, D_MODEL, tf), lambda i, j: (l, 0, j)),
            pl.BlockSpec((None, D_MODEL, tf), lambda i, j: (l, 0, j)),
            pl.BlockSpec((None, tf, D_MODEL), lambda i, j: (l, j, 0)),
        ],
        out_specs=out_specs,
        out_shape=out_shape,
        compiler_params=_params(("arbitrary", "arbitrary"), 60),
        name="ffn",
    )(hb, x, g_next, wg, wu, wd)


def _rope_tables(pos):
    half = ROPE_DIM // 2
    inv_freq = ROPE_THETA ** (-jnp.arange(half, dtype=F32) * 2.0 / ROPE_DIM)
    ang = pos[:, None] * inv_freq[None, :]
    cos, sin = jnp.cos(ang), jnp.sin(ang)
    n = pos.shape[0]
    rest = HEAD_DIM - ROPE_DIM
    zeros_h = jnp.zeros((n, half), F32)
    cos_t = jnp.concatenate([cos, cos, jnp.ones((n, rest), F32)], axis=1)
    sa_t = jnp.concatenate([-sin, zeros_h, jnp.zeros((n, rest), F32)], axis=1)
    sb_t = jnp.concatenate([zeros_h, sin, jnp.zeros((n, rest), F32)], axis=1)
    reps = LANES // HEAD_DIM
    return tuple(jnp.tile(a, (1, reps)) for a in (cos_t, sa_t, sb_t))


def _head_pool_matrices():
    head = jnp.arange(QK_DIM, dtype=jnp.int32) // HEAD_DIM
    onehot = head[:, None] == jnp.arange(LANES, dtype=jnp.int32)[None, :]
    e = jnp.where(onehot, 1.0 / HEAD_DIM, 0.0).astype(BF16)
    et = jnp.where(onehot.T, 1.0, 0.0).astype(BF16)
    return e, et


def kernel(x_prompt, x_sample, state_conv, cache_k, cache_v, norm_mix_g, w_in, w_dw, b_dw, conv_ln_g,
           conv_ln_b, w_conv_out, q_norm_g, k_norm_g, sinks, w_attn_out, w_out, norm_ffn_g, w_gate_up, w_down):
    nb, t = x_sample.shape[0], x_sample.shape[1]
    seq = x_prompt.shape[1]

    def cut(w, a, b):
        return w[:, :, a:b].astype(BF16)

    c1 = CONV_DIM
    c2 = c1 + CONV_DIM
    c3 = c2 + QK_DIM + KV_DIM
    c4 = c3 + D_MODEL
    w_lin, w_gate, w_qkv = cut(w_in, 0, c1), cut(w_in, c1, c2), cut(w_in, c2, c3)
    w_gc, w_ga = cut(w_in, c3, c4), cut(w_in, c4, c4 + D_MODEL)
    w_ffg, w_ffu = cut(w_gate_up, 0, D_FF), cut(w_gate_up, D_FF, 2 * D_FF)
    w_co, w_ao, w_o, w_dn = (w.astype(BF16) for w in (w_conv_out, w_attn_out, w_out, w_down))

    g_mix = norm_mix_g.reshape(DEPTH, 1, D_MODEL)
    g_ffn = norm_ffn_g.reshape(DEPTH, 1, D_MODEL)
    b_conv = b_dw.reshape(DEPTH, 1, CONV_DIM)
    ln_g = conv_ln_g.reshape(DEPTH, 1, CONV_DIM)
    ln_b = conv_ln_b.reshape(DEPTH, 1, CONV_DIM)
    qk_w = jnp.concatenate([jnp.tile(q_norm_g, (1, N_Q_HEADS)), jnp.tile(k_norm_g, (1, N_KV_HEADS))], axis=1)
    qk_w = qk_w.reshape(DEPTH, 1, QK_DIM)
    e, et = _head_pool_matrices()

    tabs_p = _rope_tables(jnp.arange(seq, dtype=F32))
    tabs_s = _rope_tables(jnp.tile(jnp.arange(t, dtype=F32) + PAST_LEN, nb))
    ck = cache_k.reshape(DEPTH, nb, WINDOW, KV_DIM)
    cv = cache_v.reshape(DEPTH, nb, WINDOW, KV_DIM)

    def token_tail(x, hb, c, o, l):
        mix = _merge(hb, c, o, w_gc, w_ga, w_co, w_ao, l)
        x1, hb2 = _outproj(x, mix, w_o, g_ffn, l)
        out = _ffn(hb2, x1, g_mix, w_ffg, w_ffu, w_dn, l, l + 1 < DEPTH)
        return (out[0], out[1]) if l + 1 < DEPTH else (out[0], None)

    xp = x_prompt.reshape(seq, D_MODEL)
    xs = x_sample.reshape(nb * t, D_MODEL)
    hbp = _norm(xp, g_mix, 0)
    hbs = _norm(xs, g_mix, 0)
    hist_p, k_p, v_p, hist_s, k_s, v_s = [], [], [], [], [], []
    for l in range(DEPTH):
        u3 = _glu(hbp, w_lin, w_gate, l)
        q, k, v = _qkv(hbp, w_qkv, tabs_p, qk_w, e, et, l)
        c = _conv(u3, w_dw, b_conv, ln_g, ln_b, l)
        o = _attn(q, k, v, sinks[l])
        xp, hbp_next = token_tail(xp, hbp, c, o, l)
        hist_p.append(u3[:, seq - (CONV_WIDTH - 1):, :].transpose(1, 0, 2).reshape(1, CONV_WIDTH - 1, CONV_DIM))
        k_p.append(k[seq - WINDOW:].reshape(1, WINDOW, N_KV_HEADS, HEAD_DIM))
        v_p.append(v[seq - WINDOW:].reshape(1, WINDOW, N_KV_HEADS, HEAD_DIM))
        hbp = hbp_next

        u3 = _glu(hbs, w_lin, w_gate, l)
        q, k, v = _qkv(hbs, w_qkv, tabs_s, qk_w, e, et, l)
        u = u3.transpose(1, 0, 2).reshape(nb, t, CONV_DIM)
        c, nh = _conv_s(u, state_conv, w_dw, b_conv, ln_g, ln_b, l)
        o, nk, nv = _attn_s(q, k, v, ck, cv, sinks[l], l, t)
        xs, hbs_next = token_tail(xs, hbs, c.reshape(nb * t, CONV_DIM), o, l)
        hist_s.append(nh)
        k_s.append(nk.reshape(nb, WINDOW, N_KV_HEADS, HEAD_DIM))
        v_s.append(nv.reshape(nb, WINDOW, N_KV_HEADS, HEAD_DIM))
        hbs = hbs_next

    return (xp.reshape(x_prompt.shape), xs.reshape(x_sample.shape),
            jnp.stack(hist_p), jnp.stack(k_p), jnp.stack(v_p),
            jnp.stack(hist_s), jnp.stack(k_s), jnp.stack(v_s))
```

```python
import functools

import jax
import jax.numpy as jnp
from jax import lax
from jax.experimental import pallas as pl
from jax.experimental.pallas import tpu as pltpu

F32 = jnp.float32
BF16 = jnp.bfloat16

D_MODEL = 2048
DEPTH = 4
CHUNK = 64
CONV_DIM = D_MODEL // 2
CONV_WIDTH = 31
HEAD_DIM = 64
N_Q_HEADS = (D_MODEL // 2) // HEAD_DIM
N_KV_HEADS = 4
GROUP = N_Q_HEADS // N_KV_HEADS
ATTN_DIM = N_Q_HEADS * HEAD_DIM
KV_DIM = N_KV_HEADS * HEAD_DIM
QK_DIM = ATTN_DIM + KV_DIM
WINDOW = 128
ROPE_DIM = HEAD_DIM // 4
ROPE_THETA = 500000.0
D_FF = ((8 * D_MODEL + 3 * 256 - 1) // (3 * 256)) * 256
EPS = 1e-6
PAST_LEN = 4096
LOG2E = 1.4426950408889634
Q_SCALE = (HEAD_DIM ** -0.5) * LOG2E

LANES = 128
SUBLANES = 8
MIB = 1024 * 1024
CONV_TILES = CONV_DIM // LANES

_NT = (((1,), (1,)), ((), ()))


def _params(sem, vmem_mib):
    return pltpu.CompilerParams(dimension_semantics=sem, vmem_limit_bytes=vmem_mib * MIB)


def _row_tile(m):
    return min(m, 1024)


def _parts(rows, n):
    step = rows // n
    return [(p * step, (p + 1) * step) for p in range(n)]


def _dot(a, b):
    return jnp.dot(a, b, preferred_element_type=F32)


def _rms_to_bf16(x, g):
    ms = jnp.mean(x * x, axis=-1, keepdims=True)
    return (x * lax.rsqrt(ms + EPS) * g).astype(BF16)


def _norm_kernel(x_ref, g_ref, hb_ref):
    hb_ref[...] = _rms_to_bf16(x_ref[...], g_ref[...])


def _norm(x, g, l):
    m = x.shape[0]
    tm = min(m, 512)
    return pl.pallas_call(
        _norm_kernel,
        grid=(m // tm,),
        in_specs=[
            pl.BlockSpec((tm, D_MODEL), lambda i: (i, 0)),
            pl.BlockSpec((None, 1, D_MODEL), lambda i: (l, 0, 0)),
        ],
        out_specs=pl.BlockSpec((tm, D_MODEL), lambda i: (i, 0)),
        out_shape=jax.ShapeDtypeStruct((m, D_MODEL), BF16),
        compiler_params=_params(("arbitrary",), 32),
        name="norm",
    )(x, g)


GLU_TN = 1024
GLU_PARTS = 4


def _glu_kernel(hb_ref, wl_ref, wg_ref, u_ref, *, tm):
    parts = _parts(tm, GLU_PARTS)
    ab = []
    for r0, r1 in parts:
        h = hb_ref[r0:r1, :]
        ab.append((_dot(h, wl_ref[...]), _dot(h, wg_ref[...])))
    for (r0, r1), (a, b) in zip(parts, ab):
        u = a * jax.nn.sigmoid(b)
        for t in range(GLU_TN // LANES):
            u_ref[t, r0:r1, :] = u[:, t * LANES:(t + 1) * LANES]


def _glu(hb, wl, wg, l):
    m = hb.shape[0]
    tm, tn = _row_tile(m), GLU_TN
    return pl.pallas_call(
        functools.partial(_glu_kernel, tm=tm),
        grid=(m // tm, CONV_DIM // tn),
        in_specs=[
            pl.BlockSpec((tm, D_MODEL), lambda i, j: (i, 0)),
            pl.BlockSpec((None, D_MODEL, tn), lambda i, j: (l, 0, j)),
            pl.BlockSpec((None, D_MODEL, tn), lambda i, j: (l, 0, j)),
        ],
        out_specs=pl.BlockSpec((tn // LANES, tm, LANES), lambda i, j: (j, i, 0)),
        out_shape=jax.ShapeDtypeStruct((CONV_TILES, m, LANES), F32),
        compiler_params=_params(("arbitrary", "arbitrary"), 48),
        name="glu",
    )(hb, wl, wg)


QKV_PART_ROWS = 256


def _split_bf16(a):
    hi = a.astype(BF16)
    lo = (a - hi.astype(F32)).astype(BF16)
    return hi, lo


def _qkv_kernel(hb_ref, w_ref, cos_ref, sa_ref, sb_ref, nw_ref, e_ref, et_ref, q_ref, k_ref, v_ref, *, tm):
    parts = _parts(tm, tm // QKV_PART_ROWS)
    zs = [_dot(hb_ref[r0:r1, :], w_ref[...]) for r0, r1 in parts]
    for (r0, r1), z in zip(parts, zs):
        v_ref[r0:r1, :] = z[:, QK_DIM:]
        qk = z[:, :QK_DIM]
        hi, lo = _split_bf16(qk * qk)
        e = e_ref[...]
        ms = _dot(hi, e) + _dot(lo, e)
        rhi, rlo = _split_bf16(lax.rsqrt(ms + EPS))
        et = et_ref[...]
        scale = _dot(rhi, et) + _dot(rlo, et)
        y = qk * scale * nw_ref[...]
        cos, sa, sb = cos_ref[r0:r1, :], sa_ref[r0:r1, :], sb_ref[r0:r1, :]
        for cg in range(QK_DIM // LANES):
            yc = y[:, cg * LANES:(cg + 1) * LANES]
            rot = yc * cos + pltpu.roll(yc, LANES - ROPE_DIM // 2, 1) * sa + pltpu.roll(yc, ROPE_DIM // 2, 1) * sb
            if cg < ATTN_DIM // LANES:
                q_ref[r0:r1, cg * LANES:(cg + 1) * LANES] = (rot * Q_SCALE).astype(BF16)
            else:
                c0 = cg * LANES - ATTN_DIM
                k_ref[r0:r1, c0:c0 + LANES] = rot


def _qkv(hb, w, tabs, nw, e, et, l):
    m = hb.shape[0]
    tm = _row_tile(m)
    cos, sa, sb = tabs
    tab_spec = pl.BlockSpec((tm, LANES), lambda i: (i, 0))
    return pl.pallas_call(
        functools.partial(_qkv_kernel, tm=tm),
        grid=(m // tm,),
        in_specs=[
            pl.BlockSpec((tm, D_MODEL), lambda i: (i, 0)),
            pl.BlockSpec((None, D_MODEL, QK_DIM + KV_DIM), lambda i: (l, 0, 0)),
            tab_spec, tab_spec, tab_spec,
            pl.BlockSpec((None, 1, QK_DIM), lambda i: (l, 0, 0)),
            pl.BlockSpec((QK_DIM, LANES), lambda i: (0, 0)),
            pl.BlockSpec((LANES, QK_DIM), lambda i: (0, 0)),
        ],
        out_specs=[
            pl.BlockSpec((tm, ATTN_DIM), lambda i: (i, 0)),
            pl.BlockSpec((tm, KV_DIM), lambda i: (i, 0)),
            pl.BlockSpec((tm, KV_DIM), lambda i: (i, 0)),
        ],
        out_shape=[
            jax.ShapeDtypeStruct((m, ATTN_DIM), BF16),
            jax.ShapeDtypeStruct((m, KV_DIM), F32),
            jax.ShapeDtypeStruct((m, KV_DIM), F32),
        ],
        compiler_params=_params(("arbitrary",), 56),
        name="qkv",
    )(hb, w, cos, sa, sb, nw, e, et)


CONV_BLOCK = 512
CONV_ROWS = 64
LN_ROWS = 32
HALO = 32


def _conv_kernel(u_ref, halo_ref, w_ref, b_ref, lg_ref, lb_ref, c_ref, up_ref, y_ref):
    i = pl.program_id(0)

    @pl.when(i == 0)
    def _():
        up_ref[:, 0:HALO, :] = jnp.zeros((CONV_TILES, HALO, LANES), F32)

    @pl.when(i > 0)
    def _():
        up_ref[:, 0:HALO, :] = halo_ref[...]

    up_ref[:, HALO:, :] = u_ref[...]
    first = HALO - (CONV_WIDTH - 1)
    groups = CONV_ROWS // SUBLANES
    for ct in range(CONV_TILES):
        taps = [w_ref[ct, pl.ds(j, SUBLANES, stride=0), :] for j in range(CONV_WIDTH)]
        bias = b_ref[ct, pl.ds(0, SUBLANES, stride=0), :]
        for rc in range(CONV_BLOCK // CONV_ROWS):
            acc = jnp.broadcast_to(bias, (groups, SUBLANES, LANES))
            for j in range(CONV_WIDTH):
                s = rc * CONV_ROWS + first + j
                acc = acc + taps[j] * up_ref[ct, s:s + CONV_ROWS, :].reshape(groups, SUBLANES, LANES)
            y_ref[ct, rc * CONV_ROWS:(rc + 1) * CONV_ROWS, :] = acc.reshape(CONV_ROWS, LANES)

    lg, lb = lg_ref[...], lb_ref[...]
    for r in range(CONV_BLOCK // LN_ROWS):
        r0 = r * LN_ROWS
        y = y_ref[:, r0:r0 + LN_ROWS, :]
        mean = jnp.sum(jnp.sum(y, axis=0, keepdims=True), axis=2, keepdims=True) * (1.0 / CONV_DIM)
        yc = y - mean
        var = jnp.sum(jnp.sum(yc * yc, axis=0, keepdims=True), axis=2, keepdims=True) * (1.0 / CONV_DIM)
        z = yc * lax.rsqrt(var + EPS) * lg + lb
        z = (z * jax.nn.sigmoid(z)).astype(BF16)
        for ct in range(CONV_TILES):
            c_ref[r0:r0 + LN_ROWS, ct * LANES:(ct + 1) * LANES] = z[ct]


def _conv(u3, w, b, lg, lb, l):
    m = u3.shape[1]
    tr = CONV_BLOCK
    per = tr // HALO
    tiled = lambda a: a.reshape(DEPTH, CONV_TILES, 1, LANES)
    w_tiles = w.reshape(DEPTH, CONV_WIDTH, CONV_TILES, LANES).transpose(0, 2, 1, 3)
    tvec = pl.BlockSpec((None, CONV_TILES, 1, LANES), lambda i: (l, 0, 0, 0))
    return pl.pallas_call(
        _conv_kernel,
        grid=(m // tr,),
        in_specs=[
            pl.BlockSpec((CONV_TILES, tr, LANES), lambda i: (0, i, 0)),
            pl.BlockSpec((CONV_TILES, HALO, LANES), lambda i: (0, jnp.maximum(i * per - 1, 0), 0)),
            pl.BlockSpec((None, CONV_TILES, CONV_WIDTH, LANES), lambda i: (l, 0, 0, 0)),
            tvec, tvec, tvec,
        ],
        out_specs=pl.BlockSpec((tr, CONV_DIM), lambda i: (i, 0)),
        out_shape=jax.ShapeDtypeStruct((m, CONV_DIM), BF16),
        scratch_shapes=[
            pltpu.VMEM((CONV_TILES, HALO + tr, LANES), F32),
            pltpu.VMEM((CONV_TILES, tr, LANES), F32),
        ],
        compiler_params=_params(("arbitrary",), 32),
        name="conv",
    )(u3, u3, w_tiles, tiled(b), tiled(lg), tiled(lb))


CONV_S_BATCH = 8


def _ln_swish_bf16(acc, lg, lb):
    mean = jnp.mean(acc, axis=-1, keepdims=True)
    xc = acc - mean
    var = jnp.mean(xc * xc, axis=-1, keepdims=True)
    y = xc * lax.rsqrt(var + EPS) * lg + lb
    return (y * jax.nn.sigmoid(y)).astype(BF16)


def _conv_s_kernel(u_ref, hist_ref, w_ref, b_ref, lg_ref, lb_ref, c_ref, nh_ref, up_ref, *, t):
    nhist = CONV_WIDTH - 1
    for bb in range(CONV_S_BATCH):
        up_ref[0:nhist, :] = hist_ref[bb]
        up_ref[nhist:nhist + t, :] = u_ref[bb]
        acc = jnp.broadcast_to(b_ref[...], (t, CONV_DIM))
        for j in range(CONV_WIDTH):
            acc = acc + w_ref[j:j + 1, :] * up_ref[j:j + t, :]
        c_ref[bb] = _ln_swish_bf16(acc, lg_ref[...], lb_ref[...])
        nh_ref[bb] = up_ref[t:t + nhist, :]


def _conv_s(u, hist, w, b, lg, lb, l):
    nb, t, _ = u.shape
    nhist = CONV_WIDTH - 1
    vec = pl.BlockSpec((None, 1, CONV_DIM), lambda i: (l, 0, 0))
    return pl.pallas_call(
        functools.partial(_conv_s_kernel, t=t),
        grid=(nb // CONV_S_BATCH,),
        in_specs=[
            pl.BlockSpec((CONV_S_BATCH, t, CONV_DIM), lambda i: (i, 0, 0)),
            pl.BlockSpec((None, CONV_S_BATCH, nhist, CONV_DIM), lambda i: (l, i, 0, 0)),
            pl.BlockSpec((None, CONV_WIDTH, CONV_DIM), lambda i: (l, 0, 0)),
            vec, vec, vec,
        ],
        out_specs=[
            pl.BlockSpec((CONV_S_BATCH, t, CONV_DIM), lambda i: (i, 0, 0)),
            pl.BlockSpec((CONV_S_BATCH, nhist, CONV_DIM), lambda i: (i, 0, 0)),
        ],
        out_shape=[
            jax.ShapeDtypeStruct((nb, t, CONV_DIM), BF16),
            jax.ShapeDtypeStruct((nb, nhist, CONV_DIM), F32),
        ],
        scratch_shapes=[pltpu.VMEM((nhist + t + 2, CONV_DIM), F32)],
        compiler_params=_params(("arbitrary",), 32),
        name="conv_s",
    )(u, hist, w, b, lg, lb)


ATT_Q = 2 * CHUNK
ATT_K = 4 * CHUNK
ATT_BLOCK = 1024
V_ROWS = HEAD_DIM + 16


def _sink_row(sink_ref, kh, cols_per_head):
    n = GROUP * cols_per_head
    lane_head = lax.broadcasted_iota(jnp.int32, (1, n), 1) // cols_per_head
    sink = jnp.full((1, n), sink_ref[kh * GROUP + GROUP - 1], F32)
    for g in range(GROUP - 2, -1, -1):
        sink = jnp.where(lane_head <= g, sink_ref[kh * GROUP + g], sink)
    return sink * LOG2E


def _attn_kernel(sink_ref, q_ref, k_ref, kh_ref, v_ref, vh_ref, o_ref, kbuf, vtbuf, otbuf):
    i = pl.program_id(0)
    kbuf[0:WINDOW, :] = kh_ref[...].astype(BF16)
    kbuf[WINDOW:, :] = k_ref[...].astype(BF16)
    vt_h = vh_ref[...].T.astype(BF16)
    vt_m = v_ref[...].T.astype(BF16)
    for kh in range(N_KV_HEADS):
        vtbuf[kh * V_ROWS:kh * V_ROWS + HEAD_DIM, 0:WINDOW] = vt_h[kh * HEAD_DIM:(kh + 1) * HEAD_DIM]
        vtbuf[kh * V_ROWS:kh * V_ROWS + HEAD_DIM, WINDOW:] = vt_m[kh * HEAD_DIM:(kh + 1) * HEAD_DIM]
        vtbuf[kh * V_ROWS + HEAD_DIM:(kh + 1) * V_ROWS, :] = jnp.ones((V_ROWS - HEAD_DIM, WINDOW + ATT_BLOCK), BF16)
    qt = q_ref[...].T

    n = GROUP * ATT_Q
    lane = lax.broadcasted_iota(jnp.int32, (CHUNK, n), 1)
    first_chunk_rows = (lane % ATT_Q) < CHUNK
    neg = -jnp.inf
    sinks = [_sink_row(sink_ref, kh, ATT_Q) for kh in range(N_KV_HEADS)]

    def scores(t):
        r0 = t * ATT_Q
        out = []
        for kh in range(N_KV_HEADS):
            kk = kbuf[r0:r0 + ATT_K, kh * HEAD_DIM:(kh + 1) * HEAD_DIM]
            qtt = jnp.concatenate(
                [qt[(kh * GROUP + g) * HEAD_DIM:(kh * GROUP + g + 1) * HEAD_DIM, r0:r0 + ATT_Q] for g in range(GROUP)],
                axis=1)
            out.append(_dot(kk, qtt))
        return out

    def weighted_values(t, es, mxs):
        r0 = t * ATT_Q
        for kh in range(N_KV_HEADS):
            vt = vtbuf[kh * V_ROWS:(kh + 1) * V_ROWS, r0:r0 + ATT_K]
            ot = _dot(vt, es[kh])
            den = ot[HEAD_DIM:HEAD_DIM + 1] + jnp.exp2(sinks[kh] - mxs[kh])
            ot = ot[0:HEAD_DIM] / den
            for g in range(GROUP):
                h = kh * GROUP + g
                otbuf[h * HEAD_DIM:(h + 1) * HEAD_DIM, :] = ot[:, g * ATT_Q:(g + 1) * ATT_Q]
        o_ref[r0:r0 + ATT_Q, :] = otbuf[...].T.astype(BF16)

    n_tiles = ATT_BLOCK // ATT_Q
    ahead = scores(0)
    for t in range(n_tiles):
        sts = ahead
        if t + 1 < n_tiles:
            ahead = scores(t + 1)
        es, mxs = [], []
        for kh in range(N_KV_HEADS):
            st = sts[kh]
            c0 = jnp.where(first_chunk_rows, st[0:CHUNK], neg)
            c1 = st[CHUNK:2 * CHUNK]
            c3 = jnp.where(first_chunk_rows, neg, st[3 * CHUNK:])
            if t == 0:
                c0 = jnp.where(i > 0, c0, neg)
                c1 = jnp.where(i > 0, c1, neg)
            st = jnp.concatenate([c0, c1, st[2 * CHUNK:3 * CHUNK], c3], axis=0)
            mx = jnp.maximum(jnp.max(st, axis=0, keepdims=True), sinks[kh])
            es.append(jnp.exp2(st - mx).astype(BF16))
            mxs.append(mx)
        weighted_values(t, es, mxs)


def _attn(q, k, v, sinks):
    m = q.shape[0]
    per = ATT_BLOCK // WINDOW
    main = pl.BlockSpec((ATT_BLOCK, KV_DIM), lambda i: (i, 0))
    halo = pl.BlockSpec((WINDOW, KV_DIM), lambda i: (jnp.maximum(i * per - 1, 0), 0))
    return pl.pallas_call(
        _attn_kernel,
        grid=(m // ATT_BLOCK,),
        in_specs=[
            pl.BlockSpec(memory_space=pltpu.SMEM),
            pl.BlockSpec((ATT_BLOCK, ATTN_DIM), lambda i: (i, 0)),
            main, halo, main, halo,
        ],
        out_specs=pl.BlockSpec((ATT_BLOCK, ATTN_DIM), lambda i: (i, 0)),
        out_shape=jax.ShapeDtypeStruct((m, ATTN_DIM), BF16),
        scratch_shapes=[
            pltpu.VMEM((WINDOW + ATT_BLOCK, KV_DIM), BF16),
            pltpu.VMEM((N_KV_HEADS * V_ROWS, WINDOW + ATT_BLOCK), BF16),
            pltpu.VMEM((ATTN_DIM, ATT_Q), F32),
        ],
        compiler_params=_params(("arbitrary",), 32),
        name="attn",
    )(sinks, q, k, k, v, v)


ATT_S_BATCH = 4


def _stack_heads(q_ref, r0, rows, kh):
    parts = [q_ref[r0:r0 + rows, (kh * GROUP + g) * HEAD_DIM:(kh * GROUP + g + 1) * HEAD_DIM] for g in range(GROUP)]
    return jnp.concatenate(parts, axis=0)


def _sink_column(sink_ref, kh, rows_per_head):
    n = GROUP * rows_per_head
    row = lax.broadcasted_iota(jnp.int32, (n, 1), 0)
    col = jnp.full((n, 1), sink_ref[kh * GROUP + GROUP - 1], F32)
    for g in range(GROUP - 2, -1, -1):
        col = jnp.where(row < (g + 1) * rows_per_head, sink_ref[kh * GROUP + g], col)
    return col * LOG2E


def _attn_s_kernel(sink_ref, q_ref, k_ref, v_ref, ck_ref, cv_ref, o_ref, nk_ref, nv_ref, *, t):
    keep = WINDOW - t
    sinks = [_sink_column(sink_ref, kh, t) for kh in range(N_KV_HEADS)]
    scores = []
    for bb in range(ATT_S_BATCH):
        r0 = bb * t
        nk_ref[bb, 0:keep, :] = ck_ref[bb, t:, :]
        nk_ref[bb, keep:, :] = k_ref[r0:r0 + t, :]
        nv_ref[bb, 0:keep, :] = cv_ref[bb, t:, :]
        nv_ref[bb, keep:, :] = v_ref[r0:r0 + t, :]
        for kh in range(N_KV_HEADS):
            cols = slice(kh * HEAD_DIM, (kh + 1) * HEAD_DIM)
            qs = _stack_heads(q_ref, r0, t, kh)
            s1 = lax.dot_general(qs, ck_ref[bb, :, cols].astype(BF16), _NT, preferred_element_type=F32)
            s2 = lax.dot_general(qs, k_ref[r0:r0 + t, cols].astype(BF16), _NT, preferred_element_type=F32)
            scores.append((s1, s2))
    probs = []
    for bb in range(ATT_S_BATCH):
        for kh in range(N_KV_HEADS):
            s1, s2 = scores[bb * N_KV_HEADS + kh]
            sink = sinks[kh]
            mx = jnp.maximum(jnp.maximum(jnp.max(s1, axis=-1, keepdims=True), jnp.max(s2, axis=-1, keepdims=True)), sink)
            e1 = jnp.exp2(s1 - mx)
            e2 = jnp.exp2(s2 - mx)
            den = jnp.sum(e1, axis=-1, keepdims=True) + jnp.sum(e2, axis=-1, keepdims=True) + jnp.exp2(sink - mx)
            probs.append((e1.astype(BF16), e2.astype(BF16), den))
    for bb in range(ATT_S_BATCH):
        r0 = bb * t
        for kh in range(N_KV_HEADS):
            cols = slice(kh * HEAD_DIM, (kh + 1) * HEAD_DIM)
            e1, e2, den = probs[bb * N_KV_HEADS + kh]
            o = _dot(e1, cv_ref[bb, :, cols].astype(BF16)) + _dot(e2, v_ref[r0:r0 + t, cols].astype(BF16))
            o = o / den
            o = jnp.concatenate([o[g * t:(g + 1) * t] for g in range(GROUP)], axis=1)
            o_ref[r0:r0 + t, kh * GROUP * HEAD_DIM:(kh + 1) * GROUP * HEAD_DIM] = o.astype(BF16)


def _attn_s(q, k, v, cache_k, cache_v, sinks, l, t):
    nb = q.shape[0] // t
    rows = ATT_S_BATCH * t
    new = pl.BlockSpec((rows, KV_DIM), lambda b: (b, 0))
    cache = pl.BlockSpec((None, ATT_S_BATCH, WINDOW, KV_DIM), lambda b: (l, b, 0, 0))
    out_cache = pl.BlockSpec((ATT_S_BATCH, WINDOW, KV_DIM), lambda b: (b, 0, 0))
    return pl.pallas_call(
        functools.partial(_attn_s_kernel, t=t),
        grid=(nb // ATT_S_BATCH,),
        in_specs=[
            pl.BlockSpec(memory_space=pltpu.SMEM),
            pl.BlockSpec((rows, ATTN_DIM), lambda b: (b, 0)),
            new, new, cache, cache,
        ],
        out_specs=[pl.BlockSpec((rows, ATTN_DIM), lambda b: (b, 0)), out_cache, out_cache],
        out_shape=[
            jax.ShapeDtypeStruct((nb * t, ATTN_DIM), BF16),
            jax.ShapeDtypeStruct((nb, WINDOW, KV_DIM), F32),
            jax.ShapeDtypeStruct((nb, WINDOW, KV_DIM), F32),
        ],
        compiler_params=_params(("arbitrary",), 32),
        name="attn_s",
    )(sinks, q, k, v, cache_k, cache_v)


MERGE_PARTS = 1
MERGE_TN = 512


def _merge_kernel(hb_ref, c_ref, o_ref, wgc_ref, wga_ref, wco_ref, wao_ref, mix_ref, *, tm):
    parts = _parts(tm, MERGE_PARTS)
    prods = []
    for r0, r1 in parts:
        h = hb_ref[r0:r1, :]
        prods.append((_dot(h, wgc_ref[...]), _dot(c_ref[r0:r1, :], wco_ref[...]),
                      _dot(h, wga_ref[...]), _dot(o_ref[r0:r1, :], wao_ref[...])))
    for (r0, r1), (gc, conv_out, ga, attn_out) in zip(parts, prods):
        mix = jax.nn.sigmoid(gc) * conv_out + jax.nn.sigmoid(ga) * attn_out
        mix_ref[r0:r1, :] = mix.astype(BF16)


def _merge(hb, c, o, wgc, wga, wco, wao, l):
    m = hb.shape[0]
    tm, tn = _row_tile(m), MERGE_TN
    wide = pl.BlockSpec((None, D_MODEL, tn), lambda i, j: (l, 0, j))
    narrow = pl.BlockSpec((None, CONV_DIM, tn), lambda i, j: (l, 0, j))
    rows = pl.BlockSpec((tm, CONV_DIM), lambda i, j: (i, 0))
    return pl.pallas_call(
        functools.partial(_merge_kernel, tm=tm),
        grid=(m // tm, D_MODEL // tn),
        in_specs=[pl.BlockSpec((tm, D_MODEL), lambda i, j: (i, 0)), rows, rows, wide, wide, narrow, narrow],
        out_specs=pl.BlockSpec((tm, tn), lambda i, j: (i, j)),
        out_shape=jax.ShapeDtypeStruct((m, D_MODEL), BF16),
        compiler_params=_params(("arbitrary", "arbitrary"), 48),
        name="merge",
    )(hb, c, o, wgc, wga, wco, wao)


OUT_PARTS = 2


def _outproj_kernel(x_ref, mix_ref, w_ref, g_ref, y_ref, hb_ref, *, tm):
    parts = _parts(tm, OUT_PARTS)
    prods = [_dot(mix_ref[r0:r1, :], w_ref[...]) for r0, r1 in parts]
    for (r0, r1), p in zip(parts, prods):
        y = x_ref[r0:r1, :] + p
        y_ref[r0:r1, :] = y
        hb_ref[r0:r1, :] = _rms_to_bf16(y, g_ref[...])


def _outproj(x, mix, w, g, l):
    m = x.shape[0]
    tm = min(m, 512)
    row = lambda i: (i, 0)
    return pl.pallas_call(
        functools.partial(_outproj_kernel, tm=tm),
        grid=(m // tm,),
        in_specs=[
            pl.BlockSpec((tm, D_MODEL), row),
            pl.BlockSpec((tm, D_MODEL), row),
            pl.BlockSpec((None, D_MODEL, D_MODEL), lambda i: (l, 0, 0)),
            pl.BlockSpec((None, 1, D_MODEL), lambda i: (l, 0, 0)),
        ],
        out_specs=[pl.BlockSpec((tm, D_MODEL), row), pl.BlockSpec((tm, D_MODEL), row)],
        out_shape=[jax.ShapeDtypeStruct((m, D_MODEL), F32), jax.ShapeDtypeStruct((m, D_MODEL), BF16)],
        compiler_params=_params(("arbitrary",), 56),
        name="outproj",
    )(x, mix, w, g)


FFN_TF = 512
RES_TILES = D_MODEL // FFN_TF


def _ffn_kernel(hb_ref, x_ref, g_ref, wg_ref, wu_ref, wd_ref, y_ref, *hbn_ref):
    j = pl.program_id(1)

    @pl.when(j == 0)
    def _():
        y_ref[...] = jnp.zeros(y_ref.shape, F32)

    for jj in range(RES_TILES):
        @pl.when(j == jj)
        def _():
            y_ref[:, jj * FFN_TF:(jj + 1) * FFN_TF] += x_ref[...]

    h = hb_ref[...]
    gate = _dot(h, wg_ref[...])
    up = _dot(h, wu_ref[...])
    act = (gate * jax.nn.sigmoid(gate) * up).astype(BF16)
    y_ref[...] += _dot(act, wd_ref[...])

    if hbn_ref:
        @pl.when(j == pl.num_programs(1) - 1)
        def _():
            hbn_ref[0][...] = _rms_to_bf16(y_ref[...], g_ref[...])


def _ffn(hb, x, g_next, wg, wu, wd, l, emit_next):
    m = x.shape[0]
    tm, tf = _row_tile(m), FFN_TF
    row = lambda i, j: (i, 0)
    ln = min(l + 1, DEPTH - 1)
    out_specs = [pl.BlockSpec((tm, D_MODEL), row)]
    out_shape = [jax.ShapeDtypeStruct((m, D_MODEL), F32)]
    if emit_next:
        out_specs.append(pl.BlockSpec((tm, D_MODEL), row))
        out_shape.append(jax.ShapeDtypeStruct((m, D_MODEL), BF16))
    return pl.pallas_call(
        _ffn_kernel,
        grid=(m // tm, D_FF // tf),
        in_specs=[
            pl.BlockSpec((tm, D_MODEL), row),
            pl.BlockSpec((tm, tf), lambda i, j: (i, jnp.minimum(j, RES_TILES - 1))),
            pl.BlockSpec((None, 1, D_MODEL), lambda i, j: (ln, 0, 0)),
            pl.BlockSpec((None, D_MODEL, tf), lambda i, j: (l, 0, j)),
            pl.BlockSpec((None, D_MODEL, tf), lambda i, j: (l, 0, j)),
            pl.BlockSpec((None, tf, D_MODEL), lambda i, j: (l, j, 0)),
        ],
        out_specs=out_specs,
        out_shape=out_shape,
        compiler_params=_params(("arbitrary", "arbitrary"), 60),
        name="ffn",
    )(hb, x, g_next, wg, wu, wd)


def _rope_tables(pos):
    half = ROPE_DIM // 2
    inv_freq = ROPE_THETA ** (-jnp.arange(half, dtype=F32) * 2.0 / ROPE_DIM)
    ang = pos[:, None] * inv_freq[None, :]
    cos, sin = jnp.cos(ang), jnp.sin(ang)
    n = pos.shape[0]
    rest = HEAD_DIM - ROPE_DIM
    zeros_h = jnp.zeros((n, half), F32)
    cos_t = jnp.concatenate([cos, cos, jnp.ones((n, rest), F32)], axis=1)
    sa_t = jnp.concatenate([-sin, zeros_h, jnp.zeros((n, rest), F32)], axis=1)
    sb_t = jnp.concatenate([zeros_h, sin, jnp.zeros((n, rest), F32)], axis=1)
    reps = LANES // HEAD_DIM
    return tuple(jnp.tile(a, (1, reps)) for a in (cos_t, sa_t, sb_t))


def _head_pool_matrices():
    head = jnp.arange(QK_DIM, dtype=jnp.int32) // HEAD_DIM
    onehot = head[:, None] == jnp.arange(LANES, dtype=jnp.int32)[None, :]
    e = jnp.where(onehot, 1.0 / HEAD_DIM, 0.0).astype(BF16)
    et = jnp.where(onehot.T, 1.0, 0.0).astype(BF16)
    return e, et


def kernel(x_prompt, x_sample, state_conv, cache_k, cache_v, norm_mix_g, w_in, w_dw, b_dw, conv_ln_g,
           conv_ln_b, w_conv_out, q_norm_g, k_norm_g, sinks, w_attn_out, w_out, norm_ffn_g, w_gate_up, w_down):
    nb, t = x_sample.shape[0], x_sample.shape[1]
    seq = x_prompt.shape[1]

    def cut(w, a, b):
        return w[:, :, a:b].astype(BF16)

    c1 = CONV_DIM
    c2 = c1 + CONV_DIM
    c3 = c2 + QK_DIM + KV_DIM
    c4 = c3 + D_MODEL
    w_lin, w_gate, w_qkv = cut(w_in, 0, c1), cut(w_in, c1, c2), cut(w_in, c2, c3)
    w_gc, w_ga = cut(w_in, c3, c4), cut(w_in, c4, c4 + D_MODEL)
    w_ffg, w_ffu = cut(w_gate_up, 0, D_FF), cut(w_gate_up, D_FF, 2 * D_FF)
    w_co, w_ao, w_o, w_dn = (w.astype(BF16) for w in (w_conv_out, w_attn_out, w_out, w_down))

    g_mix = norm_mix_g.reshape(DEPTH, 1, D_MODEL)
    g_ffn = norm_ffn_g.reshape(DEPTH, 1, D_MODEL)
    b_conv = b_dw.reshape(DEPTH, 1, CONV_DIM)
    ln_g = conv_ln_g.reshape(DEPTH, 1, CONV_DIM)
    ln_b = conv_ln_b.reshape(DEPTH, 1, CONV_DIM)
    qk_w = jnp.concatenate([jnp.tile(q_norm_g, (1, N_Q_HEADS)), jnp.tile(k_norm_g, (1, N_KV_HEADS))], axis=1)
    qk_w = qk_w.reshape(DEPTH, 1, QK_DIM)
    e, et = _head_pool_matrices()

    tabs_p = _rope_tables(jnp.arange(seq, dtype=F32))
    tabs_s = _rope_tables(jnp.tile(jnp.arange(t, dtype=F32) + PAST_LEN, nb))
    ck = cache_k.reshape(DEPTH, nb, WINDOW, KV_DIM)
    cv = cache_v.reshape(DEPTH, nb, WINDOW, KV_DIM)

    def token_tail(x, hb, c, o, l):
        mix = _merge(hb, c, o, w_gc, w_ga, w_co, w_ao, l)
        x1, hb2 = _outproj(x, mix, w_o, g_ffn, l)
        out = _ffn(hb2, x1, g_mix, w_ffg, w_ffu, w_dn, l, l + 1 < DEPTH)
        return (out[0], out[1]) if l + 1 < DEPTH else (out[0], None)

    xp = x_prompt.reshape(seq, D_MODEL)
    xs = x_sample.reshape(nb * t, D_MODEL)
    hbp = _norm(xp, g_mix, 0)
    hbs = _norm(xs, g_mix, 0)
    hist_p, k_p, v_p, hist_s, k_s, v_s = [], [], [], [], [], []
    for l in range(DEPTH):
        u3 = _glu(hbp, w_lin, w_gate, l)
        q, k, v = _qkv(hbp, w_qkv, tabs_p, qk_w, e, et, l)
        c = _conv(u3, w_dw, b_conv, ln_g, ln_b, l)
        o = _attn(q, k, v, sinks[l])
        xp, hbp_next = token_tail(xp, hbp, c, o, l)
        hist_p.append(u3[:, seq - (CONV_WIDTH - 1):, :].transpose(1, 0, 2).reshape(1, CONV_WIDTH - 1, CONV_DIM))
        k_p.append(k[seq - WINDOW:].reshape(1, WINDOW, N_KV_HEADS, HEAD_DIM))
        v_p.append(v[seq - WINDOW:].reshape(1, WINDOW, N_KV_HEADS, HEAD_DIM))
        hbp = hbp_next

        u3 = _glu(hbs, w_lin, w_gate, l)
        q, k, v = _qkv(hbs, w_qkv, tabs_s, qk_w, e, et, l)
        u = u3.transpose(1, 0, 2).reshape(nb, t, CONV_DIM)
        c, nh = _conv_s(u, state_conv, w_dw, b_conv, ln_g, ln_b, l)
        o, nk, nv = _attn_s(q, k, v, ck, cv, sinks[l], l, t)
        xs, hbs_next = token_tail(xs, hbs, c.reshape(nb * t, CONV_DIM), o, l)
        hist_s.append(nh)
        k_s.append(nk.reshape(nb, WINDOW, N_KV_HEADS, HEAD_DIM))
        v_s.append(nv.reshape(nb, WINDOW, N_KV_HEADS, HEAD_DIM))
        hbs = hbs_next

    return (xp.reshape(x_prompt.shape), xs.reshape(x_sample.shape),
            jnp.stack(hist_p), jnp.stack(k_p), jnp.stack(v_p),
            jnp.stack(hist_s), jnp.stack(k_s), jnp.stack(v_s))
```

```python
import functools

import jax
import jax.numpy as jnp
from jax import lax
from jax.experimental import pallas as pl
from jax.experimental.pallas import tpu as pltpu

F32 = jnp.float32
BF16 = jnp.bfloat16

D_MODEL = 2048
DEPTH = 4
CHUNK = 64
CONV_DIM = D_MODEL // 2
CONV_WIDTH = 31
HEAD_DIM = 64
N_Q_HEADS = (D_MODEL // 2) // HEAD_DIM
N_KV_HEADS = 4
GROUP = N_Q_HEADS // N_KV_HEADS
ATTN_DIM = N_Q_HEADS * HEAD_DIM
KV_DIM = N_KV_HEADS * HEAD_DIM
QK_DIM = ATTN_DIM + KV_DIM
WINDOW = 128
ROPE_DIM = HEAD_DIM // 4
ROPE_THETA = 500000.0
D_FF = ((8 * D_MODEL + 3 * 256 - 1) // (3 * 256)) * 256
EPS = 1e-6
PAST_LEN = 4096
LOG2E = 1.4426950408889634
Q_SCALE = (HEAD_DIM ** -0.5) * LOG2E

LANES = 128
SUBLANES = 8
MIB = 1024 * 1024
CONV_TILES = CONV_DIM // LANES

_NT = (((1,), (1,)), ((), ()))


def _params(sem, vmem_mib):
    return pltpu.CompilerParams(dimension_semantics=sem, vmem_limit_bytes=vmem_mib * MIB)


def _row_tile(m):
    return min(m, 1024)


def _parts(rows, n):
    step = rows // n
    return [(p * step, (p + 1) * step) for p in range(n)]


def _dot(a, b):
    return jnp.dot(a, b, preferred_element_type=F32)


def _rms_to_bf16(x, g):
    ms = jnp.mean(x * x, axis=-1, keepdims=True)
    return (x * lax.rsqrt(ms + EPS) * g).astype(BF16)


def _norm_kernel(x_ref, g_ref, hb_ref):
    hb_ref[...] = _rms_to_bf16(x_ref[...], g_ref[...])


def _norm(x, g, l):
    m = x.shape[0]
    tm = min(m, 512)
    return pl.pallas_call(
        _norm_kernel,
        grid=(m // tm,),
        in_specs=[
            pl.BlockSpec((tm, D_MODEL), lambda i: (i, 0)),
            pl.BlockSpec((None, 1, D_MODEL), lambda i: (l, 0, 0)),
        ],
        out_specs=pl.BlockSpec((tm, D_MODEL), lambda i: (i, 0)),
        out_shape=jax.ShapeDtypeStruct((m, D_MODEL), BF16),
        compiler_params=_params(("arbitrary",), 32),
        name="norm",
    )(x, g)


GLU_TN = 1024
GLU_PARTS = 4


def _glu_kernel(hb_ref, wl_ref, wg_ref, u_ref, *, tm):
    parts = _parts(tm, GLU_PARTS)
    ab = []
    for r0, r1 in parts:
        h = hb_ref[r0:r1, :]
        ab.append((_dot(h, wl_ref[...]), _dot(h, wg_ref[...])))
    for (r0, r1), (a, b) in zip(parts, ab):
        u = a * jax.nn.sigmoid(b)
        for t in range(GLU_TN // LANES):
            u_ref[t, r0:r1, :] = u[:, t * LANES:(t + 1) * LANES]


def _glu(hb, wl, wg, l):
    m = hb.shape[0]
    tm, tn = _row_tile(m), GLU_TN
    return pl.pallas_call(
        functools.partial(_glu_kernel, tm=tm),
        grid=(m // tm, CONV_DIM // tn),
        in_specs=[
            pl.BlockSpec((tm, D_MODEL), lambda i, j: (i, 0)),
            pl.BlockSpec((None, D_MODEL, tn), lambda i, j: (l, 0, j)),
            pl.BlockSpec((None, D_MODEL, tn), lambda i, j: (l, 0, j)),
        ],
        out_specs=pl.BlockSpec((tn // LANES, tm, LANES), lambda i, j: (j, i, 0)),
        out_shape=jax.ShapeDtypeStruct((CONV_TILES, m, LANES), F32),
        compiler_params=_params(("arbitrary", "arbitrary"), 48),
        name="glu",
    )(hb, wl, wg)


QKV_PART_ROWS = 256


def _split_bf16(a):
    hi = a.astype(BF16)
    lo = (a - hi.astype(F32)).astype(BF16)
    return hi, lo


def _qkv_kernel(hb_ref, w_ref, cos_ref, sa_ref, sb_ref, nw_ref, e_ref, et_ref, q_ref, k_ref, v_ref, *, tm):
    parts = _parts(tm, tm // QKV_PART_ROWS)
    zs = [_dot(hb_ref[r0:r1, :], w_ref[...]) for r0, r1 in parts]
    for (r0, r1), z in zip(parts, zs):
        v_ref[r0:r1, :] = z[:, QK_DIM:]
        qk = z[:, :QK_DIM]
        hi, lo = _split_bf16(qk * qk)
        e = e_ref[...]
        ms = _dot(hi, e) + _dot(lo, e)
        rhi, rlo = _split_bf16(lax.rsqrt(ms + EPS))
        et = et_ref[...]
        scale = _dot(rhi, et) + _dot(rlo, et)
        y = qk * scale * nw_ref[...]
        cos, sa, sb = cos_ref[r0:r1, :], sa_ref[r0:r1, :], sb_ref[r0:r1, :]
        for cg in range(QK_DIM // LANES):
            yc = y[:, cg * LANES:(cg + 1) * LANES]
            rot = yc * cos + pltpu.roll(yc, LANES - ROPE_DIM // 2, 1) * sa + pltpu.roll(yc, ROPE_DIM // 2, 1) * sb
            if cg < ATTN_DIM // LANES:
                q_ref[r0:r1, cg * LANES:(cg + 1) * LANES] = (rot * Q_SCALE).astype(BF16)
            else:
                c0 = cg * LANES - ATTN_DIM
                k_ref[r0:r1, c0:c0 + LANES] = rot


def _qkv(hb, w, tabs, nw, e, et, l):
    m = hb.shape[0]
    tm = _row_tile(m)
    cos, sa, sb = tabs
    tab_spec = pl.BlockSpec((tm, LANES), lambda i: (i, 0))
    return pl.pallas_call(
        functools.partial(_qkv_kernel, tm=tm),
        grid=(m // tm,),
        in_specs=[
            pl.BlockSpec((tm, D_MODEL), lambda i: (i, 0)),
            pl.BlockSpec((None, D_MODEL, QK_DIM + KV_DIM), lambda i: (l, 0, 0)),
            tab_spec, tab_spec, tab_spec,
            pl.BlockSpec((None, 1, QK_DIM), lambda i: (l, 0, 0)),
            pl.BlockSpec((QK_DIM, LANES), lambda i: (0, 0)),
            pl.BlockSpec((LANES, QK_DIM), lambda i: (0, 0)),
        ],
        out_specs=[
            pl.BlockSpec((tm, ATTN_DIM), lambda i: (i, 0)),
            pl.BlockSpec((tm, KV_DIM), lambda i: (i, 0)),
            pl.BlockSpec((tm, KV_DIM), lambda i: (i, 0)),
        ],
        out_shape=[
            jax.ShapeDtypeStruct((m, ATTN_DIM), BF16),
            jax.ShapeDtypeStruct((m, KV_DIM), F32),
            jax.ShapeDtypeStruct((m, KV_DIM), F32),
        ],
        compiler_params=_params(("arbitrary",), 56),
        name="qkv",
    )(hb, w, cos, sa, sb, nw, e, et)


CONV_BLOCK = 512
CONV_ROWS = 64
LN_ROWS = 32
HALO = 32


def _conv_kernel(u_ref, halo_ref, w_ref, b_ref, lg_ref, lb_ref, c_ref, up_ref, y_ref):
    i = pl.program_id(0)

    @pl.when(i == 0)
    def _():
        up_ref[:, 0:HALO, :] = jnp.zeros((CONV_TILES, HALO, LANES), F32)

    @pl.when(i > 0)
    def _():
        up_ref[:, 0:HALO, :] = halo_ref[...]

    up_ref[:, HALO:, :] = u_ref[...]
    first = HALO - (CONV_WIDTH - 1)
    groups = CONV_ROWS // SUBLANES
    for ct in range(CONV_TILES):
        taps = [w_ref[ct, pl.ds(j, SUBLANES, stride=0), :] for j in range(CONV_WIDTH)]
        bias = b_ref[ct, pl.ds(0, SUBLANES, stride=0), :]
        for rc in range(CONV_BLOCK // CONV_ROWS):
            acc = jnp.broadcast_to(bias, (groups, SUBLANES, LANES))
            for j in range(CONV_WIDTH):
                s = rc * CONV_ROWS + first + j
                acc = acc + taps[j] * up_ref[ct, s:s + CONV_ROWS, :].reshape(groups, SUBLANES, LANES)
            y_ref[ct, rc * CONV_ROWS:(rc + 1) * CONV_ROWS, :] = acc.reshape(CONV_ROWS, LANES)

    lg, lb = lg_ref[...], lb_ref[...]
    for r in range(CONV_BLOCK // LN_ROWS):
        r0 = r * LN_ROWS
        y = y_ref[:, r0:r0 + LN_ROWS, :]
        mean = jnp.sum(jnp.sum(y, axis=0, keepdims=True), axis=2, keepdims=True) * (1.0 / CONV_DIM)
        yc = y - mean
        var = jnp.sum(jnp.sum(yc * yc, axis=0, keepdims=True), axis=2, keepdims=True) * (1.0 / CONV_DIM)
        z = yc * lax.rsqrt(var + EPS) * lg + lb
        z = (z * jax.nn.sigmoid(z)).astype(BF16)
        for ct in range(CONV_TILES):
            c_ref[r0:r0 + LN_ROWS, ct * LANES:(ct + 1) * LANES] = z[ct]


def _conv(u3, w, b, lg, lb, l):
    m = u3.shape[1]
    tr = CONV_BLOCK
    per = tr // HALO
    tiled = lambda a: a.reshape(DEPTH, CONV_TILES, 1, LANES)
    w_tiles = w.reshape(DEPTH, CONV_WIDTH, CONV_TILES, LANES).transpose(0, 2, 1, 3)
    tvec = pl.BlockSpec((None, CONV_TILES, 1, LANES), lambda i: (l, 0, 0, 0))
    return pl.pallas_call(
        _conv_kernel,
        grid=(m // tr,),
        in_specs=[
            pl.BlockSpec((CONV_TILES, tr, LANES), lambda i: (0, i, 0)),
            pl.BlockSpec((CONV_TILES, HALO, LANES), lambda i: (0, jnp.maximum(i * per - 1, 0), 0)),
            pl.BlockSpec((None, CONV_TILES, CONV_WIDTH, LANES), lambda i: (l, 0, 0, 0)),
            tvec, tvec, tvec,
        ],
        out_specs=pl.BlockSpec((tr, CONV_DIM), lambda i: (i, 0)),
        out_shape=jax.ShapeDtypeStruct((m, CONV_DIM), BF16),
        scratch_shapes=[
            pltpu.VMEM((CONV_TILES, HALO + tr, LANES), F32),
            pltpu.VMEM((CONV_TILES, tr, LANES), F32),
        ],
        compiler_params=_params(("arbitrary",), 32),
        name="conv",
    )(u3, u3, w_tiles, tiled(b), tiled(lg), tiled(lb))


CONV_S_BATCH = 8


def _ln_swish_bf16(acc, lg, lb):
    mean = jnp.mean(acc, axis=-1, keepdims=True)
    xc = acc - mean
    var = jnp.mean(xc * xc, axis=-1, keepdims=True)
    y = xc * lax.rsqrt(var + EPS) * lg + lb
    return (y * jax.nn.sigmoid(y)).astype(BF16)


def _conv_s_kernel(u_ref, hist_ref, w_ref, b_ref, lg_ref, lb_ref, c_ref, nh_ref, up_ref, *, t):
    nhist = CONV_WIDTH - 1
    for bb in range(CONV_S_BATCH):
        up_ref[0:nhist, :] = hist_ref[bb]
        up_ref[nhist:nhist + t, :] = u_ref[bb]
        acc = jnp.broadcast_to(b_ref[...], (t, CONV_DIM))
        for j in range(CONV_WIDTH):
            acc = acc + w_ref[j:j + 1, :] * up_ref[j:j + t, :]
        c_ref[bb] = _ln_swish_bf16(acc, lg_ref[...], lb_ref[...])
        nh_ref[bb] = up_ref[t:t + nhist, :]


def _conv_s(u, hist, w, b, lg, lb, l):
    nb, t, _ = u.shape
    nhist = CONV_WIDTH - 1
    vec = pl.BlockSpec((None, 1, CONV_DIM), lambda i: (l, 0, 0))
    return pl.pallas_call(
        functools.partial(_conv_s_kernel, t=t),
        grid=(nb // CONV_S_BATCH,),
        in_specs=[
            pl.BlockSpec((CONV_S_BATCH, t, CONV_DIM), lambda i: (i, 0, 0)),
            pl.BlockSpec((None, CONV_S_BATCH, nhist, CONV_DIM), lambda i: (l, i, 0, 0)),
            pl.BlockSpec((None, CONV_WIDTH, CONV_DIM), lambda i: (l, 0, 0)),
            vec, vec, vec,
        ],
        out_specs=[
            pl.BlockSpec((CONV_S_BATCH, t, CONV_DIM), lambda i: (i, 0, 0)),
            pl.BlockSpec((CONV_S_BATCH, nhist, CONV_DIM), lambda i: (i, 0, 0)),
        ],
        out_shape=[
            jax.ShapeDtypeStruct((nb, t, CONV_DIM), BF16),
            jax.ShapeDtypeStruct((nb, nhist, CONV_DIM), F32),
        ],
        scratch_shapes=[pltpu.VMEM((nhist + t + 2, CONV_DIM), F32)],
        compiler_params=_params(("arbitrary",), 32),
        name="conv_s",
    )(u, hist, w, b, lg, lb)


ATT_Q = 2 * CHUNK
ATT_K = 4 * CHUNK
ATT_BLOCK = 1024
V_ROWS = HEAD_DIM + 16


def _sink_row(sink_ref, kh, cols_per_head):
    n = GROUP * cols_per_head
    lane_head = lax.broadcasted_iota(jnp.int32, (1, n), 1) // cols_per_head
    sink = jnp.full((1, n), sink_ref[kh * GROUP + GROUP - 1], F32)
    for g in range(GROUP - 2, -1, -1):
        sink = jnp.where(lane_head <= g, sink_ref[kh * GROUP + g], sink)
    return sink * LOG2E


def _attn_kernel(sink_ref, q_ref, k_ref, kh_ref, v_ref, vh_ref, o_ref, kbuf, vtbuf, otbuf):
    i = pl.program_id(0)
    kbuf[0:WINDOW, :] = kh_ref[...].astype(BF16)
    kbuf[WINDOW:, :] = k_ref[...].astype(BF16)
    vt_h = vh_ref[...].T.astype(BF16)
    vt_m = v_ref[...].T.astype(BF16)
    for kh in range(N_KV_HEADS):
        vtbuf[kh * V_ROWS:kh * V_ROWS + HEAD_DIM, 0:WINDOW] = vt_h[kh * HEAD_DIM:(kh + 1) * HEAD_DIM]
        vtbuf[kh * V_ROWS:kh * V_ROWS + HEAD_DIM, WINDOW:] = vt_m[kh * HEAD_DIM:(kh + 1) * HEAD_DIM]
        vtbuf[kh * V_ROWS + HEAD_DIM:(kh + 1) * V_ROWS, :] = jnp.ones((V_ROWS - HEAD_DIM, WINDOW + ATT_BLOCK), BF16)
    qt = q_ref[...].T

    n = GROUP * ATT_Q
    lane = lax.broadcasted_iota(jnp.int32, (CHUNK, n), 1)
    first_chunk_rows = (lane % ATT_Q) < CHUNK
    neg = -jnp.inf
    sinks = [_sink_row(sink_ref, kh, ATT_Q) for kh in range(N_KV_HEADS)]

    def scores(t):
        r0 = t * ATT_Q
        out = []
        for kh in range(N_KV_HEADS):
            kk = kbuf[r0:r0 + ATT_K, kh * HEAD_DIM:(kh + 1) * HEAD_DIM]
            qtt = jnp.concatenate(
                [qt[(kh * GROUP + g) * HEAD_DIM:(kh * GROUP + g + 1) * HEAD_DIM, r0:r0 + ATT_Q] for g in range(GROUP)],
                axis=1)
            out.append(_dot(kk, qtt))
        return out

    def weighted_values(t, es, mxs):
        r0 = t * ATT_Q
        for kh in range(N_KV_HEADS):
            vt = vtbuf[kh * V_ROWS:(kh + 1) * V_ROWS, r0:r0 + ATT_K]
            ot = _dot(vt, es[kh])
            den = ot[HEAD_DIM:HEAD_DIM + 1] + jnp.exp2(sinks[kh] - mxs[kh])
            ot = ot[0:HEAD_DIM] / den
            for g in range(GROUP):
                h = kh * GROUP + g
                otbuf[h * HEAD_DIM:(h + 1) * HEAD_DIM, :] = ot[:, g * ATT_Q:(g + 1) * ATT_Q]
        o_ref[r0:r0 + ATT_Q, :] = otbuf[...].T.astype(BF16)

    n_tiles = ATT_BLOCK // ATT_Q
    ahead = scores(0)
    for t in range(n_tiles):
        sts = ahead
        if t + 1 < n_tiles:
            ahead = scores(t + 1)
        es, mxs = [], []
        for kh in range(N_KV_HEADS):
            st = sts[kh]
            c0 = jnp.where(first_chunk_rows, st[0:CHUNK], neg)
            c1 = st[CHUNK:2 * CHUNK]
            c3 = jnp.where(first_chunk_rows, neg, st[3 * CHUNK:])
            if t == 0:
                c0 = jnp.where(i > 0, c0, neg)
                c1 = jnp.where(i > 0, c1, neg)
            st = jnp.concatenate([c0, c1, st[2 * CHUNK:3 * CHUNK], c3], axis=0)
            mx = jnp.maximum(jnp.max(st, axis=0, keepdims=True), sinks[kh])
            es.append(jnp.exp2(st - mx).astype(BF16))
            mxs.append(mx)
        weighted_values(t, es, mxs)


def _attn(q, k, v, sinks):
    m = q.shape[0]
    per = ATT_BLOCK // WINDOW
    main = pl.BlockSpec((ATT_BLOCK, KV_DIM), lambda i: (i, 0))
    halo = pl.BlockSpec((WINDOW, KV_DIM), lambda i: (jnp.maximum(i * per - 1, 0), 0))
    return pl.pallas_call(
        _attn_kernel,
        grid=(m // ATT_BLOCK,),
        in_specs=[
            pl.BlockSpec(memory_space=pltpu.SMEM),
            pl.BlockSpec((ATT_BLOCK, ATTN_DIM), lambda i: (i, 0)),
            main, halo, main, halo,
        ],
        out_specs=pl.BlockSpec((ATT_BLOCK, ATTN_DIM), lambda i: (i, 0)),
        out_shape=jax.ShapeDtypeStruct((m, ATTN_DIM), BF16),
        scratch_shapes=[
            pltpu.VMEM((WINDOW + ATT_BLOCK, KV_DIM), BF16),
            pltpu.VMEM((N_KV_HEADS * V_ROWS, WINDOW + ATT_BLOCK), BF16),
            pltpu.VMEM((ATTN_DIM, ATT_Q), F32),
        ],
        compiler_params=_params(("arbitrary",), 32),
        name="attn",
    )(sinks, q, k, k, v, v)


ATT_S_BATCH = 4


def _stack_heads(q_ref, r0, rows, kh):
    parts = [q_ref[r0:r0 + rows, (kh * GROUP + g) * HEAD_DIM:(kh * GROUP + g + 1) * HEAD_DIM] for g in range(GROUP)]
    return jnp.concatenate(parts, axis=0)


def _sink_column(sink_ref, kh, rows_per_head):
    n = GROUP * rows_per_head
    row = lax.broadcasted_iota(jnp.int32, (n, 1), 0)
    col = jnp.full((n, 1), sink_ref[kh * GROUP + GROUP - 1], F32)
    for g in range(GROUP - 2, -1, -1):
        col = jnp.where(row < (g + 1) * rows_per_head, sink_ref[kh * GROUP + g], col)
    return col * LOG2E


def _attn_s_kernel(sink_ref, q_ref, k_ref, v_ref, ck_ref, cv_ref, o_ref, nk_ref, nv_ref, *, t):
    keep = WINDOW - t
    sinks = [_sink_column(sink_ref, kh, t) for kh in range(N_KV_HEADS)]
    scores = []
    for bb in range(ATT_S_BATCH):
        r0 = bb * t
        nk_ref[bb, 0:keep, :] = ck_ref[bb, t:, :]
        nk_ref[bb, keep:, :] = k_ref[r0:r0 + t, :]
        nv_ref[bb, 0:keep, :] = cv_ref[bb, t:, :]
        nv_ref[bb, keep:, :] = v_ref[r0:r0 + t, :]
        for kh in range(N_KV_HEADS):
            cols = slice(kh * HEAD_DIM, (kh + 1) * HEAD_DIM)
            qs = _stack_heads(q_ref, r0, t, kh)
            s1 = lax.dot_general(qs, ck_ref[bb, :, cols].astype(BF16), _NT, preferred_element_type=F32)
            s2 = lax.dot_general(qs, k_ref[r0:r0 + t, cols].astype(BF16), _NT, preferred_element_type=F32)
            scores.append((s1, s2))
    probs = []
    for bb in range(ATT_S_BATCH):
        for kh in range(N_KV_HEADS):
            s1, s2 = scores[bb * N_KV_HEADS + kh]
            sink = sinks[kh]
            mx = jnp.maximum(jnp.maximum(jnp.max(s1, axis=-1, keepdims=True), jnp.max(s2, axis=-1, keepdims=True)), sink)
            e1 = jnp.exp2(s1 - mx)
            e2 = jnp.exp2(s2 - mx)
            den = jnp.sum(e1, axis=-1, keepdims=True) + jnp.sum(e2, axis=-1, keepdims=True) + jnp.exp2(sink - mx)
            probs.append((e1.astype(BF16), e2.astype(BF16), den))
    for bb in range(ATT_S_BATCH):
        r0 = bb * t
        for kh in range(N_KV_HEADS):
            cols = slice(kh * HEAD_DIM, (kh + 1) * HEAD_DIM)
            e1, e2, den = probs[bb * N_KV_HEADS + kh]
            o = _dot(e1, cv_ref[bb, :, cols].astype(BF16)) + _dot(e2, v_ref[r0:r0 + t, cols].astype(BF16))
            o = o / den
            o = jnp.concatenate([o[g * t:(g + 1) * t] for g in range(GROUP)], axis=1)
            o_ref[r0:r0 + t, kh * GROUP * HEAD_DIM:(kh + 1) * GROUP * HEAD_DIM] = o.astype(BF16)


def _attn_s(q, k, v, cache_k, cache_v, sinks, l, t):
    nb = q.shape[0] // t
    rows = ATT_S_BATCH * t
    new = pl.BlockSpec((rows, KV_DIM), lambda b: (b, 0))
    cache = pl.BlockSpec((None, ATT_S_BATCH, WINDOW, KV_DIM), lambda b: (l, b, 0, 0))
    out_cache = pl.BlockSpec((ATT_S_BATCH, WINDOW, KV_DIM), lambda b: (b, 0, 0))
    return pl.pallas_call(
        functools.partial(_attn_s_kernel, t=t),
        grid=(nb // ATT_S_BATCH,),
        in_specs=[
            pl.BlockSpec(memory_space=pltpu.SMEM),
            pl.BlockSpec((rows, ATTN_DIM), lambda b: (b, 0)),
            new, new, cache, cache,
        ],
        out_specs=[pl.BlockSpec((rows, ATTN_DIM), lambda b: (b, 0)), out_cache, out_cache],
        out_shape=[
            jax.ShapeDtypeStruct((nb * t, ATTN_DIM), BF16),
            jax.ShapeDtypeStruct((nb, WINDOW, KV_DIM), F32),
            jax.ShapeDtypeStruct((nb, WINDOW, KV_DIM), F32),
        ],
        compiler_params=_params(("arbitrary",), 32),
        name="attn_s",
    )(sinks, q, k, v, cache_k, cache_v)


MERGE_PARTS = 1
MERGE_TN = 512


def _merge_kernel(hb_ref, c_ref, o_ref, wgc_ref, wga_ref, wco_ref, wao_ref, mix_ref, *, tm):
    parts = _parts(tm, MERGE_PARTS)
    prods = []
    for r0, r1 in parts:
        h = hb_ref[r0:r1, :]
        prods.append((_dot(h, wgc_ref[...]), _dot(c_ref[r0:r1, :], wco_ref[...]),
                      _dot(h, wga_ref[...]), _dot(o_ref[r0:r1, :], wao_ref[...])))
    for (r0, r1), (gc, conv_out, ga, attn_out) in zip(parts, prods):
        mix = jax.nn.sigmoid(gc) * conv_out + jax.nn.sigmoid(ga) * attn_out
        mix_ref[r0:r1, :] = mix.astype(BF16)


def _merge(hb, c, o, wgc, wga, wco, wao, l):
    m = hb.shape[0]
    tm = _row_tile(m)
    tn = MERGE_TN if m > tm else 2 * MERGE_TN
    wide = pl.BlockSpec((None, D_MODEL, tn), lambda i, j: (l, 0, j))
    narrow = pl.BlockSpec((None, CONV_DIM, tn), lambda i, j: (l, 0, j))
    rows = pl.BlockSpec((tm, CONV_DIM), lambda i, j: (i, 0))
    return pl.pallas_call(
        functools.partial(_merge_kernel, tm=tm),
        grid=(m // tm, D_MODEL // tn),
        in_specs=[pl.BlockSpec((tm, D_MODEL), lambda i, j: (i, 0)), rows, rows, wide, wide, narrow, narrow],
        out_specs=pl.BlockSpec((tm, tn), lambda i, j: (i, j)),
        out_shape=jax.ShapeDtypeStruct((m, D_MODEL), BF16),
        compiler_params=_params(("arbitrary", "arbitrary"), 48),
        name="merge",
    )(hb, c, o, wgc, wga, wco, wao)


OUT_PARTS = 2


def _outproj_kernel(x_ref, mix_ref, w_ref, g_ref, y_ref, hb_ref, *, tm):
    parts = _parts(tm, OUT_PARTS)
    prods = [_dot(mix_ref[r0:r1, :], w_ref[...]) for r0, r1 in parts]
    for (r0, r1), p in zip(parts, prods):
        y = x_ref[r0:r1, :] + p
        y_ref[r0:r1, :] = y
        hb_ref[r0:r1, :] = _rms_to_bf16(y, g_ref[...])


def _outproj(x, mix, w, g, l):
    m = x.shape[0]
    tm = min(m, 512)
    row = lambda i: (i, 0)
    return pl.pallas_call(
        functools.partial(_outproj_kernel, tm=tm),
        grid=(m // tm,),
        in_specs=[
            pl.BlockSpec((tm, D_MODEL), row),
            pl.BlockSpec((tm, D_MODEL), row),
            pl.BlockSpec((None, D_MODEL, D_MODEL), lambda i: (l, 0, 0)),
            pl.BlockSpec((None, 1, D_MODEL), lambda i: (l, 0, 0)),
        ],
        out_specs=[pl.BlockSpec((tm, D_MODEL), row), pl.BlockSpec((tm, D_MODEL), row)],
        out_shape=[jax.ShapeDtypeStruct((m, D_MODEL), F32), jax.ShapeDtypeStruct((m, D_MODEL), BF16)],
        compiler_params=_params(("arbitrary",), 56),
        name="outproj",
    )(x, mix, w, g)


FFN_TF = 512
RES_TILES = D_MODEL // FFN_TF


def _ffn_kernel(hb_ref, x_ref, g_ref, wg_ref, wu_ref, wd_ref, y_ref, *hbn_ref):
    j = pl.program_id(1)

    @pl.when(j == 0)
    def _():
        y_ref[...] = jnp.zeros(y_ref.shape, F32)

    for jj in range(RES_TILES):
        @pl.when(j == jj)
        def _():
            y_ref[:, jj * FFN_TF:(jj + 1) * FFN_TF] += x_ref[...]

    h = hb_ref[...]
    gate = _dot(h, wg_ref[...])
    up = _dot(h, wu_ref[...])
    act = (gate * jax.nn.sigmoid(gate) * up).astype(BF16)
    y_ref[...] += _dot(act, wd_ref[...])

    if hbn_ref:
        @pl.when(j == pl.num_programs(1) - 1)
        def _():
            hbn_ref[0][...] = _rms_to_bf16(y_ref[...], g_ref[...])


def _ffn(hb, x, g_next, wg, wu, wd, l, emit_next):
    m = x.shape[0]
    tm, tf = _row_tile(m), FFN_TF
    row = lambda i, j: (i, 0)
    ln = min(l + 1, DEPTH - 1)
    out_specs = [pl.BlockSpec((tm, D_MODEL), row)]
    out_shape = [jax.ShapeDtypeStruct((m, D_MODEL), F32)]
    if emit_next:
        out_specs.append(pl.BlockSpec((tm, D_MODEL), row))
        out_shape.append(jax.ShapeDtypeStruct((m, D_MODEL), BF16))
    return pl.pallas_call(
        _ffn_kernel,
        grid=(m // tm, D_FF // tf),
        in_specs=[
            pl.BlockSpec((tm, D_MODEL), row),
            pl.BlockSpec((tm, tf), lambda i, j: (i, jnp.minimum(j, RES_TILES - 1))),
            pl.BlockSpec((None, 1, D_MODEL), lambda i, j: (ln, 0, 0)),
            pl.BlockSpec((None, D_MODEL, tf), lambda i, j: (l, 0, j)),
            pl.BlockSpec((None, D_MODEL, tf), lambda i, j: (l, 0, j)),
            pl.BlockSpec((None, tf, D_MODEL), lambda i, j: (l, j, 0)),
        ],
        out_specs=out_specs,
        out_shape=out_shape,
        compiler_params=_params(("arbitrary", "arbitrary"), 60),
        name="ffn",
    )(hb, x, g_next, wg, wu, wd)


def _rope_tables(pos):
    half = ROPE_DIM // 2
    inv_freq = ROPE_THETA ** (-jnp.arange(half, dtype=F32) * 2.0 / ROPE_DIM)
    ang = pos[:, None] * inv_freq[None, :]
    cos, sin = jnp.cos(ang), jnp.sin(ang)
    n = pos.shape[0]
    rest = HEAD_DIM - ROPE_DIM
    zeros_h = jnp.zeros((n, half), F32)
    cos_t = jnp.concatenate([cos, cos, jnp.ones((n, rest), F32)], axis=1)
    sa_t = jnp.concatenate([-sin, zeros_h, jnp.zeros((n, rest), F32)], axis=1)
    sb_t = jnp.concatenate([zeros_h, sin, jnp.zeros((n, rest), F32)], axis=1)
    reps = LANES // HEAD_DIM
    return tuple(jnp.tile(a, (1, reps)) for a in (cos_t, sa_t, sb_t))


def _head_pool_matrices():
    head = jnp.arange(QK_DIM, dtype=jnp.int32) // HEAD_DIM
    onehot = head[:, None] == jnp.arange(LANES, dtype=jnp.int32)[None, :]
    e = jnp.where(onehot, 1.0 / HEAD_DIM, 0.0).astype(BF16)
    et = jnp.where(onehot.T, 1.0, 0.0).astype(BF16)
    return e, et


def kernel(x_prompt, x_sample, state_conv, cache_k, cache_v, norm_mix_g, w_in, w_dw, b_dw, conv_ln_g,
           conv_ln_b, w_conv_out, q_norm_g, k_norm_g, sinks, w_attn_out, w_out, norm_ffn_g, w_gate_up, w_down):
    nb, t = x_sample.shape[0], x_sample.shape[1]
    seq = x_prompt.shape[1]

    def cut(w, a, b):
        return w[:, :, a:b].astype(BF16)

    c1 = CONV_DIM
    c2 = c1 + CONV_DIM
    c3 = c2 + QK_DIM + KV_DIM
    c4 = c3 + D_MODEL
    w_lin, w_gate, w_qkv = cut(w_in, 0, c1), cut(w_in, c1, c2), cut(w_in, c2, c3)
    w_gc, w_ga = cut(w_in, c3, c4), cut(w_in, c4, c4 + D_MODEL)
    w_ffg, w_ffu = cut(w_gate_up, 0, D_FF), cut(w_gate_up, D_FF, 2 * D_FF)
    w_co, w_ao, w_o, w_dn = (w.astype(BF16) for w in (w_conv_out, w_attn_out, w_out, w_down))

    g_mix = norm_mix_g.reshape(DEPTH, 1, D_MODEL)
    g_ffn = norm_ffn_g.reshape(DEPTH, 1, D_MODEL)
    b_conv = b_dw.reshape(DEPTH, 1, CONV_DIM)
    ln_g = conv_ln_g.reshape(DEPTH, 1, CONV_DIM)
    ln_b = conv_ln_b.reshape(DEPTH, 1, CONV_DIM)
    qk_w = jnp.concatenate([jnp.tile(q_norm_g, (1, N_Q_HEADS)), jnp.tile(k_norm_g, (1, N_KV_HEADS))], axis=1)
    qk_w = qk_w.reshape(DEPTH, 1, QK_DIM)
    e, et = _head_pool_matrices()

    tabs_p = _rope_tables(jnp.arange(seq, dtype=F32))
    tabs_s = _rope_tables(jnp.tile(jnp.arange(t, dtype=F32) + PAST_LEN, nb))
    ck = cache_k.reshape(DEPTH, nb, WINDOW, KV_DIM)
    cv = cache_v.reshape(DEPTH, nb, WINDOW, KV_DIM)

    def token_tail(x, hb, c, o, l):
        mix = _merge(hb, c, o, w_gc, w_ga, w_co, w_ao, l)
        x1, hb2 = _outproj(x, mix, w_o, g_ffn, l)
        out = _ffn(hb2, x1, g_mix, w_ffg, w_ffu, w_dn, l, l + 1 < DEPTH)
        return (out[0], out[1]) if l + 1 < DEPTH else (out[0], None)

    xp = x_prompt.reshape(seq, D_MODEL)
    xs = x_sample.reshape(nb * t, D_MODEL)
    hbp = _norm(xp, g_mix, 0)
    hbs = _norm(xs, g_mix, 0)
    hist_p, k_p, v_p, hist_s, k_s, v_s = [], [], [], [], [], []
    for l in range(DEPTH):
        u3 = _glu(hbp, w_lin, w_gate, l)
        q, k, v = _qkv(hbp, w_qkv, tabs_p, qk_w, e, et, l)
        c = _conv(u3, w_dw, b_conv, ln_g, ln_b, l)
        o = _attn(q, k, v, sinks[l])
        xp, hbp_next = token_tail(xp, hbp, c, o, l)
        hist_p.append(u3[:, seq - (CONV_WIDTH - 1):, :].transpose(1, 0, 2).reshape(1, CONV_WIDTH - 1, CONV_DIM))
        k_p.append(k[seq - WINDOW:].reshape(1, WINDOW, N_KV_HEADS, HEAD_DIM))
        v_p.append(v[seq - WINDOW:].reshape(1, WINDOW, N_KV_HEADS, HEAD_DIM))
        hbp = hbp_next

        u3 = _glu(hbs, w_lin, w_gate, l)
        q, k, v = _qkv(hbs, w_qkv, tabs_s, qk_w, e, et, l)
        u = u3.transpose(1, 0, 2).reshape(nb, t, CONV_DIM)
        c, nh = _conv_s(u, state_conv, w_dw, b_conv, ln_g, ln_b, l)
        o, nk, nv = _attn_s(q, k, v, ck, cv, sinks[l], l, t)
        xs, hbs_next = token_tail(xs, hbs, c.reshape(nb * t, CONV_DIM), o, l)
        hist_s.append(nh)
        k_s.append(nk.reshape(nb, WINDOW, N_KV_HEADS, HEAD_DIM))
        v_s.append(nv.reshape(nb, WINDOW, N_KV_HEADS, HEAD_DIM))
        hbs = hbs_next

    return (xp.reshape(x_prompt.shape), xs.reshape(x_sample.shape),
            jnp.stack(hist_p), jnp.stack(k_p), jnp.stack(v_p),
            jnp.stack(hist_s), jnp.stack(k_s), jnp.stack(v_s))
```

```python
import functools

import jax
import jax.numpy as jnp
from jax import lax
from jax.experimental import pallas as pl
from jax.experimental.pallas import tpu as pltpu

F32 = jnp.float32
BF16 = jnp.bfloat16

D_MODEL = 2048
DEPTH = 4
CHUNK = 64
CONV_DIM = D_MODEL // 2
CONV_WIDTH = 31
HEAD_DIM = 64
N_Q_HEADS = (D_MODEL // 2) // HEAD_DIM
N_KV_HEADS = 4
GROUP = N_Q_HEADS // N_KV_HEADS
ATTN_DIM = N_Q_HEADS * HEAD_DIM
KV_DIM = N_KV_HEADS * HEAD_DIM
QK_DIM = ATTN_DIM + KV_DIM
WINDOW = 128
ROPE_DIM = HEAD_DIM // 4
ROPE_THETA = 500000.0
D_FF = ((8 * D_MODEL + 3 * 256 - 1) // (3 * 256)) * 256
EPS = 1e-6
PAST_LEN = 4096
LOG2E = 1.4426950408889634
Q_SCALE = (HEAD_DIM ** -0.5) * LOG2E

LANES = 128
SUBLANES = 8
MIB = 1024 * 1024
CONV_TILES = CONV_DIM // LANES

_NT = (((1,), (1,)), ((), ()))


def _params(sem, vmem_mib, fuse_inputs=None):
    return pltpu.CompilerParams(dimension_semantics=sem, vmem_limit_bytes=vmem_mib * MIB,
                                allow_input_fusion=fuse_inputs)


def _row_tile(m):
    return min(m, 1024)


def _parts(rows, n):
    step = rows // n
    return [(p * step, (p + 1) * step) for p in range(n)]


def _dot(a, b):
    return jnp.dot(a, b, preferred_element_type=F32)


def _rms_to_bf16(x, g):
    ms = jnp.mean(x * x, axis=-1, keepdims=True)
    return (x * lax.rsqrt(ms + EPS) * g).astype(BF16)


def _norm_kernel(x_ref, g_ref, hb_ref):
    hb_ref[...] = _rms_to_bf16(x_ref[...], g_ref[...])


def _norm(x, g, l):
    m = x.shape[0]
    tm = min(m, 512)
    return pl.pallas_call(
        _norm_kernel,
        grid=(m // tm,),
        in_specs=[
            pl.BlockSpec((tm, D_MODEL), lambda i: (i, 0)),
            pl.BlockSpec((None, 1, D_MODEL), lambda i: (l, 0, 0)),
        ],
        out_specs=pl.BlockSpec((tm, D_MODEL), lambda i: (i, 0)),
        out_shape=jax.ShapeDtypeStruct((m, D_MODEL), BF16),
        compiler_params=_params(("arbitrary",), 32),
        name="norm",
    )(x, g)


GLU_TN = 1024
GLU_PARTS = 4


def _glu_kernel(hb_ref, wl_ref, wg_ref, u_ref, *, tm):
    parts = _parts(tm, GLU_PARTS)
    ab = []
    for r0, r1 in parts:
        h = hb_ref[r0:r1, :]
        ab.append((_dot(h, wl_ref[...]), _dot(h, wg_ref[...])))
    for (r0, r1), (a, b) in zip(parts, ab):
        u = a * jax.nn.sigmoid(b)
        for t in range(GLU_TN // LANES):
            u_ref[t, r0:r1, :] = u[:, t * LANES:(t + 1) * LANES]


def _glu(hb, wl, wg, l):
    m = hb.shape[0]
    tm, tn = _row_tile(m), GLU_TN
    return pl.pallas_call(
        functools.partial(_glu_kernel, tm=tm),
        grid=(m // tm, CONV_DIM // tn),
        in_specs=[
            pl.BlockSpec((tm, D_MODEL), lambda i, j: (i, 0)),
            pl.BlockSpec((None, D_MODEL, tn), lambda i, j: (l, 0, j)),
            pl.BlockSpec((None, D_MODEL, tn), lambda i, j: (l, 0, j)),
        ],
        out_specs=pl.BlockSpec((tn // LANES, tm, LANES), lambda i, j: (j, i, 0)),
        out_shape=jax.ShapeDtypeStruct((CONV_TILES, m, LANES), F32),
        compiler_params=_params(("arbitrary", "arbitrary"), 48),
        name="glu",
    )(hb, wl, wg)


QKV_PART_ROWS = 256


def _split_bf16(a):
    hi = a.astype(BF16)
    lo = (a - hi.astype(F32)).astype(BF16)
    return hi, lo


def _qkv_kernel(hb_ref, w_ref, cos_ref, sa_ref, sb_ref, nw_ref, e_ref, et_ref, q_ref, k_ref, v_ref, *, tm):
    parts = _parts(tm, tm // QKV_PART_ROWS)
    zs = [_dot(hb_ref[r0:r1, :], w_ref[...]) for r0, r1 in parts]
    for (r0, r1), z in zip(parts, zs):
        v_ref[r0:r1, :] = z[:, QK_DIM:]
        qk = z[:, :QK_DIM]
        hi, lo = _split_bf16(qk * qk)
        e = e_ref[...]
        ms = _dot(hi, e) + _dot(lo, e)
        rhi, rlo = _split_bf16(lax.rsqrt(ms + EPS))
        et = et_ref[...]
        scale = _dot(rhi, et) + _dot(rlo, et)
        y = qk * scale * nw_ref[...]
        cos, sa, sb = cos_ref[r0:r1, :], sa_ref[r0:r1, :], sb_ref[r0:r1, :]
        for cg in range(QK_DIM // LANES):
            yc = y[:, cg * LANES:(cg + 1) * LANES]
            rot = yc * cos + pltpu.roll(yc, LANES - ROPE_DIM // 2, 1) * sa + pltpu.roll(yc, ROPE_DIM // 2, 1) * sb
            if cg < ATTN_DIM // LANES:
                q_ref[r0:r1, cg * LANES:(cg + 1) * LANES] = (rot * Q_SCALE).astype(BF16)
            else:
                c0 = cg * LANES - ATTN_DIM
                k_ref[r0:r1, c0:c0 + LANES] = rot


def _qkv(hb, w, tabs, nw, e, et, l):
    m = hb.shape[0]
    tm = _row_tile(m)
    cos, sa, sb = tabs
    tab_spec = pl.BlockSpec((tm, LANES), lambda i: (i, 0))
    return pl.pallas_call(
        functools.partial(_qkv_kernel, tm=tm),
        grid=(m // tm,),
        in_specs=[
            pl.BlockSpec((tm, D_MODEL), lambda i: (i, 0)),
            pl.BlockSpec((None, D_MODEL, QK_DIM + KV_DIM), lambda i: (l, 0, 0)),
            tab_spec, tab_spec, tab_spec,
            pl.BlockSpec((None, 1, QK_DIM), lambda i: (l, 0, 0)),
            pl.BlockSpec((QK_DIM, LANES), lambda i: (0, 0)),
            pl.BlockSpec((LANES, QK_DIM), lambda i: (0, 0)),
        ],
        out_specs=[
            pl.BlockSpec((tm, ATTN_DIM), lambda i: (i, 0)),
            pl.BlockSpec((tm, KV_DIM), lambda i: (i, 0)),
            pl.BlockSpec((tm, KV_DIM), lambda i: (i, 0)),
        ],
        out_shape=[
            jax.ShapeDtypeStruct((m, ATTN_DIM), BF16),
            jax.ShapeDtypeStruct((m, KV_DIM), F32),
            jax.ShapeDtypeStruct((m, KV_DIM), F32),
        ],
        compiler_params=_params(("arbitrary",), 56, [False, True] + [False] * 6),
        name="qkv",
    )(hb, w, cos, sa, sb, nw, e, et)


CONV_BLOCK = 512
CONV_ROWS = 64
LN_ROWS = 32
HALO = 32


def _conv_kernel(u_ref, halo_ref, w_ref, b_ref, lg_ref, lb_ref, c_ref, up_ref, y_ref):
    i = pl.program_id(0)

    @pl.when(i == 0)
    def _():
        up_ref[:, 0:HALO, :] = jnp.zeros((CONV_TILES, HALO, LANES), F32)

    @pl.when(i > 0)
    def _():
        up_ref[:, 0:HALO, :] = halo_ref[...]

    up_ref[:, HALO:, :] = u_ref[...]
    first = HALO - (CONV_WIDTH - 1)
    groups = CONV_ROWS // SUBLANES
    for ct in range(CONV_TILES):
        taps = [w_ref[ct, pl.ds(j, SUBLANES, stride=0), :] for j in range(CONV_WIDTH)]
        bias = b_ref[ct, pl.ds(0, SUBLANES, stride=0), :]
        for rc in range(CONV_BLOCK // CONV_ROWS):
            acc = jnp.broadcast_to(bias, (groups, SUBLANES, LANES))
            for j in range(CONV_WIDTH):
                s = rc * CONV_ROWS + first + j
                acc = acc + taps[j] * up_ref[ct, s:s + CONV_ROWS, :].reshape(groups, SUBLANES, LANES)
            y_ref[ct, rc * CONV_ROWS:(rc + 1) * CONV_ROWS, :] = acc.reshape(CONV_ROWS, LANES)

    lg, lb = lg_ref[...], lb_ref[...]
    for r in range(CONV_BLOCK // LN_ROWS):
        r0 = r * LN_ROWS
        y = y_ref[:, r0:r0 + LN_ROWS, :]
        mean = jnp.sum(jnp.sum(y, axis=0, keepdims=True), axis=2, keepdims=True) * (1.0 / CONV_DIM)
        yc = y - mean
        var = jnp.sum(jnp.sum(yc * yc, axis=0, keepdims=True), axis=2, keepdims=True) * (1.0 / CONV_DIM)
        z = yc * lax.rsqrt(var + EPS) * lg + lb
        z = (z * jax.nn.sigmoid(z)).astype(BF16)
        for ct in range(CONV_TILES):
            c_ref[r0:r0 + LN_ROWS, ct * LANES:(ct + 1) * LANES] = z[ct]


def _conv(u3, w, b, lg, lb, l):
    m = u3.shape[1]
    tr = CONV_BLOCK
    per = tr // HALO
    tiled = lambda a: a.reshape(DEPTH, CONV_TILES, 1, LANES)
    w_tiles = w.reshape(DEPTH, CONV_WIDTH, CONV_TILES, LANES).transpose(0, 2, 1, 3)
    tvec = pl.BlockSpec((None, CONV_TILES, 1, LANES), lambda i: (l, 0, 0, 0))
    return pl.pallas_call(
        _conv_kernel,
        grid=(m // tr,),
        in_specs=[
            pl.BlockSpec((CONV_TILES, tr, LANES), lambda i: (0, i, 0)),
            pl.BlockSpec((CONV_TILES, HALO, LANES), lambda i: (0, jnp.maximum(i * per - 1, 0), 0)),
            pl.BlockSpec((None, CONV_TILES, CONV_WIDTH, LANES), lambda i: (l, 0, 0, 0)),
            tvec, tvec, tvec,
        ],
        out_specs=pl.BlockSpec((tr, CONV_DIM), lambda i: (i, 0)),
        out_shape=jax.ShapeDtypeStruct((m, CONV_DIM), BF16),
        scratch_shapes=[
            pltpu.VMEM((CONV_TILES, HALO + tr, LANES), F32),
            pltpu.VMEM((CONV_TILES, tr, LANES), F32),
        ],
        compiler_params=_params(("arbitrary",), 32),
        name="conv",
    )(u3, u3, w_tiles, tiled(b), tiled(lg), tiled(lb))


CONV_S_BATCH = 8


def _ln_swish_bf16(acc, lg, lb):
    mean = jnp.mean(acc, axis=-1, keepdims=True)
    xc = acc - mean
    var = jnp.mean(xc * xc, axis=-1, keepdims=True)
    y = xc * lax.rsqrt(var + EPS) * lg + lb
    return (y * jax.nn.sigmoid(y)).astype(BF16)


def _conv_s_kernel(u_ref, hist_ref, w_ref, b_ref, lg_ref, lb_ref, c_ref, nh_ref, up_ref, *, t):
    nhist = CONV_WIDTH - 1
    for bb in range(CONV_S_BATCH):
        up_ref[0:nhist, :] = hist_ref[bb]
        up_ref[nhist:nhist + t, :] = u_ref[bb]
        acc = jnp.broadcast_to(b_ref[...], (t, CONV_DIM))
        for j in range(CONV_WIDTH):
            acc = acc + w_ref[j:j + 1, :] * up_ref[j:j + t, :]
        c_ref[bb] = _ln_swish_bf16(acc, lg_ref[...], lb_ref[...])
        nh_ref[bb] = up_ref[t:t + nhist, :]


def _conv_s(u, hist, w, b, lg, lb, l):
    nb, t, _ = u.shape
    nhist = CONV_WIDTH - 1
    vec = pl.BlockSpec((None, 1, CONV_DIM), lambda i: (l, 0, 0))
    return pl.pallas_call(
        functools.partial(_conv_s_kernel, t=t),
        grid=(nb // CONV_S_BATCH,),
        in_specs=[
            pl.BlockSpec((CONV_S_BATCH, t, CONV_DIM), lambda i: (i, 0, 0)),
            pl.BlockSpec((None, CONV_S_BATCH, nhist, CONV_DIM), lambda i: (l, i, 0, 0)),
            pl.BlockSpec((None, CONV_WIDTH, CONV_DIM), lambda i: (l, 0, 0)),
            vec, vec, vec,
        ],
        out_specs=[
            pl.BlockSpec((CONV_S_BATCH, t, CONV_DIM), lambda i: (i, 0, 0)),
            pl.BlockSpec((CONV_S_BATCH, nhist, CONV_DIM), lambda i: (i, 0, 0)),
        ],
        out_shape=[
            jax.ShapeDtypeStruct((nb, t, CONV_DIM), BF16),
            jax.ShapeDtypeStruct((nb, nhist, CONV_DIM), F32),
        ],
        scratch_shapes=[pltpu.VMEM((nhist + t + 2, CONV_DIM), F32)],
        compiler_params=_params(("arbitrary",), 32),
        name="conv_s",
    )(u, hist, w, b, lg, lb)


ATT_Q = 2 * CHUNK
ATT_K = 4 * CHUNK
ATT_BLOCK = 1024
V_ROWS = HEAD_DIM + 16


def _sink_row(sink_ref, kh, cols_per_head):
    n = GROUP * cols_per_head
    lane_head = lax.broadcasted_iota(jnp.int32, (1, n), 1) // cols_per_head
    sink = jnp.full((1, n), sink_ref[kh * GROUP + GROUP - 1], F32)
    for g in range(GROUP - 2, -1, -1):
        sink = jnp.where(lane_head <= g, sink_ref[kh * GROUP + g], sink)
    return sink * LOG2E


def _attn_kernel(sink_ref, q_ref, k_ref, kh_ref, v_ref, vh_ref, o_ref, kbuf, vtbuf, otbuf):
    i = pl.program_id(0)
    kbuf[0:WINDOW, :] = kh_ref[...].astype(BF16)
    kbuf[WINDOW:, :] = k_ref[...].astype(BF16)
    vt_h = vh_ref[...].T.astype(BF16)
    vt_m = v_ref[...].T.astype(BF16)
    for kh in range(N_KV_HEADS):
        vtbuf[kh * V_ROWS:kh * V_ROWS + HEAD_DIM, 0:WINDOW] = vt_h[kh * HEAD_DIM:(kh + 1) * HEAD_DIM]
        vtbuf[kh * V_ROWS:kh * V_ROWS + HEAD_DIM, WINDOW:] = vt_m[kh * HEAD_DIM:(kh + 1) * HEAD_DIM]
        vtbuf[kh * V_ROWS + HEAD_DIM:(kh + 1) * V_ROWS, :] = jnp.ones((V_ROWS - HEAD_DIM, WINDOW + ATT_BLOCK), BF16)
    qt = q_ref[...].T

    n = GROUP * ATT_Q
    lane = lax.broadcasted_iota(jnp.int32, (CHUNK, n), 1)
    first_chunk_rows = (lane % ATT_Q) < CHUNK
    neg = -jnp.inf
    sinks = [_sink_row(sink_ref, kh, ATT_Q) for kh in range(N_KV_HEADS)]

    def scores(t):
        r0 = t * ATT_Q
        out = []
        for kh in range(N_KV_HEADS):
            kk = kbuf[r0:r0 + ATT_K, kh * HEAD_DIM:(kh + 1) * HEAD_DIM]
            qtt = jnp.concatenate(
                [qt[(kh * GROUP + g) * HEAD_DIM:(kh * GROUP + g + 1) * HEAD_DIM, r0:r0 + ATT_Q] for g in range(GROUP)],
                axis=1)
            out.append(_dot(kk, qtt))
        return out

    def weighted_values(t, es, mxs):
        r0 = t * ATT_Q
        for kh in range(N_KV_HEADS):
            vt = vtbuf[kh * V_ROWS:(kh + 1) * V_ROWS, r0:r0 + ATT_K]
            ot = _dot(vt, es[kh])
            den = ot[HEAD_DIM:HEAD_DIM + 1] + jnp.exp2(sinks[kh] - mxs[kh])
            ot = ot[0:HEAD_DIM] / den
            for g in range(GROUP):
                h = kh * GROUP + g
                otbuf[h * HEAD_DIM:(h + 1) * HEAD_DIM, :] = ot[:, g * ATT_Q:(g + 1) * ATT_Q]
        o_ref[r0:r0 + ATT_Q, :] = otbuf[...].T.astype(BF16)

    n_tiles = ATT_BLOCK // ATT_Q
    ahead = scores(0)
    for t in range(n_tiles):
        sts = ahead
        if t + 1 < n_tiles:
            ahead = scores(t + 1)
        es, mxs = [], []
        for kh in range(N_KV_HEADS):
            st = sts[kh]
            c0 = jnp.where(first_chunk_rows, st[0:CHUNK], neg)
            c1 = st[CHUNK:2 * CHUNK]
            c3 = jnp.where(first_chunk_rows, neg, st[3 * CHUNK:])
            if t == 0:
                c0 = jnp.where(i > 0, c0, neg)
                c1 = jnp.where(i > 0, c1, neg)
            st = jnp.concatenate([c0, c1, st[2 * CHUNK:3 * CHUNK], c3], axis=0)
            mx = jnp.maximum(jnp.max(st, axis=0, keepdims=True), sinks[kh])
            es.append(jnp.exp2(st - mx).astype(BF16))
            mxs.append(mx)
        weighted_values(t, es, mxs)


def _attn(q, k, v, sinks):
    m = q.shape[0]
    per = ATT_BLOCK // WINDOW
    main = pl.BlockSpec((ATT_BLOCK, KV_DIM), lambda i: (i, 0))
    halo = pl.BlockSpec((WINDOW, KV_DIM), lambda i: (jnp.maximum(i * per - 1, 0), 0))
    return pl.pallas_call(
        _attn_kernel,
        grid=(m // ATT_BLOCK,),
        in_specs=[
            pl.BlockSpec(memory_space=pltpu.SMEM),
            pl.BlockSpec((ATT_BLOCK, ATTN_DIM), lambda i: (i, 0)),
            main, halo, main, halo,
        ],
        out_specs=pl.BlockSpec((ATT_BLOCK, ATTN_DIM), lambda i: (i, 0)),
        out_shape=jax.ShapeDtypeStruct((m, ATTN_DIM), BF16),
        scratch_shapes=[
            pltpu.VMEM((WINDOW + ATT_BLOCK, KV_DIM), BF16),
            pltpu.VMEM((N_KV_HEADS * V_ROWS, WINDOW + ATT_BLOCK), BF16),
            pltpu.VMEM((ATTN_DIM, ATT_Q), F32),
        ],
        compiler_params=_params(("arbitrary",), 32),
        name="attn",
    )(sinks, q, k, k, v, v)


ATT_S_BATCH = 4


def _stack_heads(q_ref, r0, rows, kh):
    parts = [q_ref[r0:r0 + rows, (kh * GROUP + g) * HEAD_DIM:(kh * GROUP + g + 1) * HEAD_DIM] for g in range(GROUP)]
    return jnp.concatenate(parts, axis=0)


def _sink_column(sink_ref, kh, rows_per_head):
    n = GROUP * rows_per_head
    row = lax.broadcasted_iota(jnp.int32, (n, 1), 0)
    col = jnp.full((n, 1), sink_ref[kh * GROUP + GROUP - 1], F32)
    for g in range(GROUP - 2, -1, -1):
        col = jnp.where(row < (g + 1) * rows_per_head, sink_ref[kh * GROUP + g], col)
    return col * LOG2E


def _attn_s_kernel(sink_ref, q_ref, k_ref, v_ref, ck_ref, cv_ref, o_ref, nk_ref, nv_ref, *, t):
    keep = WINDOW - t
    sinks = [_sink_column(sink_ref, kh, t) for kh in range(N_KV_HEADS)]
    scores = []
    for bb in range(ATT_S_BATCH):
        r0 = bb * t
        nk_ref[bb, 0:keep, :] = ck_ref[bb, t:, :]
        nk_ref[bb, keep:, :] = k_ref[r0:r0 + t, :]
        nv_ref[bb, 0:keep, :] = cv_ref[bb, t:, :]
        nv_ref[bb, keep:, :] = v_ref[r0:r0 + t, :]
        for kh in range(N_KV_HEADS):
            cols = slice(kh * HEAD_DIM, (kh + 1) * HEAD_DIM)
            qs = _stack_heads(q_ref, r0, t, kh)
            s1 = lax.dot_general(qs, ck_ref[bb, :, cols].astype(BF16), _NT, preferred_element_type=F32)
            s2 = lax.dot_general(qs, k_ref[r0:r0 + t, cols].astype(BF16), _NT, preferred_element_type=F32)
            scores.append((s1, s2))
    probs = []
    for bb in range(ATT_S_BATCH):
        for kh in range(N_KV_HEADS):
            s1, s2 = scores[bb * N_KV_HEADS + kh]
            sink = sinks[kh]
            mx = jnp.maximum(jnp.maximum(jnp.max(s1, axis=-1, keepdims=True), jnp.max(s2, axis=-1, keepdims=True)), sink)
            e1 = jnp.exp2(s1 - mx)
            e2 = jnp.exp2(s2 - mx)
            den = jnp.sum(e1, axis=-1, keepdims=True) + jnp.sum(e2, axis=-1, keepdims=True) + jnp.exp2(sink - mx)
            probs.append((e1.astype(BF16), e2.astype(BF16), den))
    for bb in range(ATT_S_BATCH):
        r0 = bb * t
        for kh in range(N_KV_HEADS):
            cols = slice(kh * HEAD_DIM, (kh + 1) * HEAD_DIM)
            e1, e2, den = probs[bb * N_KV_HEADS + kh]
            o = _dot(e1, cv_ref[bb, :, cols].astype(BF16)) + _dot(e2, v_ref[r0:r0 + t, cols].astype(BF16))
            o = o / den
            o = jnp.concatenate([o[g * t:(g + 1) * t] for g in range(GROUP)], axis=1)
            o_ref[r0:r0 + t, kh * GROUP * HEAD_DIM:(kh + 1) * GROUP * HEAD_DIM] = o.astype(BF16)


def _attn_s(q, k, v, cache_k, cache_v, sinks, l, t):
    nb = q.shape[0] // t
    rows = ATT_S_BATCH * t
    new = pl.BlockSpec((rows, KV_DIM), lambda b: (b, 0))
    cache = pl.BlockSpec((None, ATT_S_BATCH, WINDOW, KV_DIM), lambda b: (l, b, 0, 0))
    out_cache = pl.BlockSpec((ATT_S_BATCH, WINDOW, KV_DIM), lambda b: (b, 0, 0))
    return pl.pallas_call(
        functools.partial(_attn_s_kernel, t=t),
        grid=(nb // ATT_S_BATCH,),
        in_specs=[
            pl.BlockSpec(memory_space=pltpu.SMEM),
            pl.BlockSpec((rows, ATTN_DIM), lambda b: (b, 0)),
            new, new, cache, cache,
        ],
        out_specs=[pl.BlockSpec((rows, ATTN_DIM), lambda b: (b, 0)), out_cache, out_cache],
        out_shape=[
            jax.ShapeDtypeStruct((nb * t, ATTN_DIM), BF16),
            jax.ShapeDtypeStruct((nb, WINDOW, KV_DIM), F32),
            jax.ShapeDtypeStruct((nb, WINDOW, KV_DIM), F32),
        ],
        compiler_params=_params(("arbitrary",), 32),
        name="attn_s",
    )(sinks, q, k, v, cache_k, cache_v)


MERGE_PARTS = 1
MERGE_TN = 512


def _merge_kernel(hb_ref, c_ref, o_ref, wgc_ref, wga_ref, wco_ref, wao_ref, mix_ref, *, tm):
    parts = _parts(tm, MERGE_PARTS)
    prods = []
    for r0, r1 in parts:
        h = hb_ref[r0:r1, :]
        prods.append((_dot(h, wgc_ref[...]), _dot(c_ref[r0:r1, :], wco_ref[...]),
                      _dot(h, wga_ref[...]), _dot(o_ref[r0:r1, :], wao_ref[...])))
    for (r0, r1), (gc, conv_out, ga, attn_out) in zip(parts, prods):
        mix = jax.nn.sigmoid(gc) * conv_out + jax.nn.sigmoid(ga) * attn_out
        mix_ref[r0:r1, :] = mix.astype(BF16)


def _merge(hb, c, o, wgc, wga, wco, wao, l):
    m = hb.shape[0]
    tm = _row_tile(m)
    tn = MERGE_TN if m > tm else 2 * MERGE_TN
    wide = pl.BlockSpec((None, D_MODEL, tn), lambda i, j: (l, 0, j))
    narrow = pl.BlockSpec((None, CONV_DIM, tn), lambda i, j: (l, 0, j))
    rows = pl.BlockSpec((tm, CONV_DIM), lambda i, j: (i, 0))
    return pl.pallas_call(
        functools.partial(_merge_kernel, tm=tm),
        grid=(m // tm, D_MODEL // tn),
        in_specs=[pl.BlockSpec((tm, D_MODEL), lambda i, j: (i, 0)), rows, rows, wide, wide, narrow, narrow],
        out_specs=pl.BlockSpec((tm, tn), lambda i, j: (i, j)),
        out_shape=jax.ShapeDtypeStruct((m, D_MODEL), BF16),
        compiler_params=_params(("arbitrary", "arbitrary"), 48),
        name="merge",
    )(hb, c, o, wgc, wga, wco, wao)


OUT_PARTS = 2


def _outproj_kernel(x_ref, mix_ref, w_ref, g_ref, y_ref, hb_ref, *, tm):
    parts = _parts(tm, OUT_PARTS)
    prods = [_dot(mix_ref[r0:r1, :], w_ref[...]) for r0, r1 in parts]
    for (r0, r1), p in zip(parts, prods):
        y = x_ref[r0:r1, :] + p
        y_ref[r0:r1, :] = y
        hb_ref[r0:r1, :] = _rms_to_bf16(y, g_ref[...])


def _outproj(x, mix, w, g, l):
    m = x.shape[0]
    tm = min(m, 512)
    row = lambda i: (i, 0)
    return pl.pallas_call(
        functools.partial(_outproj_kernel, tm=tm),
        grid=(m // tm,),
        in_specs=[
            pl.BlockSpec((tm, D_MODEL), row),
            pl.BlockSpec((tm, D_MODEL), row),
            pl.BlockSpec((None, D_MODEL, D_MODEL), lambda i: (l, 0, 0)),
            pl.BlockSpec((None, 1, D_MODEL), lambda i: (l, 0, 0)),
        ],
        out_specs=[pl.BlockSpec((tm, D_MODEL), row), pl.BlockSpec((tm, D_MODEL), row)],
        out_shape=[jax.ShapeDtypeStruct((m, D_MODEL), F32), jax.ShapeDtypeStruct((m, D_MODEL), BF16)],
        compiler_params=_params(("arbitrary",), 56, [False, False, True, False]),
        name="outproj",
    )(x, mix, w, g)


FFN_TF = 512
RES_TILES = D_MODEL // FFN_TF


def _ffn_kernel(hb_ref, x_ref, g_ref, wg_ref, wu_ref, wd_ref, y_ref, *hbn_ref):
    j = pl.program_id(1)

    @pl.when(j == 0)
    def _():
        y_ref[...] = jnp.zeros(y_ref.shape, F32)

    for jj in range(RES_TILES):
        @pl.when(j == jj)
        def _():
            y_ref[:, jj * FFN_TF:(jj + 1) * FFN_TF] += x_ref[...]

    h = hb_ref[...]
    gate = _dot(h, wg_ref[...])
    up = _dot(h, wu_ref[...])
    act = (gate * jax.nn.sigmoid(gate) * up).astype(BF16)
    y_ref[...] += _dot(act, wd_ref[...])

    if hbn_ref:
        @pl.when(j == pl.num_programs(1) - 1)
        def _():
            hbn_ref[0][...] = _rms_to_bf16(y_ref[...], g_ref[...])


def _ffn(hb, x, g_next, wg, wu, wd, l, emit_next):
    m = x.shape[0]
    tm, tf = _row_tile(m), FFN_TF
    row = lambda i, j: (i, 0)
    ln = min(l + 1, DEPTH - 1)
    out_specs = [pl.BlockSpec((tm, D_MODEL), row)]
    out_shape = [jax.ShapeDtypeStruct((m, D_MODEL), F32)]
    if emit_next:
        out_specs.append(pl.BlockSpec((tm, D_MODEL), row))
        out_shape.append(jax.ShapeDtypeStruct((m, D_MODEL), BF16))
    return pl.pallas_call(
        _ffn_kernel,
        grid=(m // tm, D_FF // tf),
        in_specs=[
            pl.BlockSpec((tm, D_MODEL), row),
            pl.BlockSpec((tm, tf), lambda i, j: (i, jnp.minimum(j, RES_TILES - 1))),
            pl.BlockSpec((None, 1, D_MODEL), lambda i, j: (ln, 0, 0)),
            pl.BlockSpec((None, D_MODEL, tf), lambda i, j: (l, 0, j)),
            pl.BlockSpec((None, D_MODEL, tf), lambda i, j: (l, 0, j)),
            pl.BlockSpec((None, tf, D_MODEL), lambda i, j: (l, j, 0)),
        ],
        out_specs=out_specs,
        out_shape=out_shape,
        compiler_params=_params(("arbitrary", "arbitrary"), 60),
        name="ffn",
    )(hb, x, g_next, wg, wu, wd)


def _rope_tables(pos):
    half = ROPE_DIM // 2
    inv_freq = ROPE_THETA ** (-jnp.arange(half, dtype=F32) * 2.0 / ROPE_DIM)
    ang = pos[:, None] * inv_freq[None, :]
    cos, sin = jnp.cos(ang), jnp.sin(ang)
    n = pos.shape[0]
    rest = HEAD_DIM - ROPE_DIM
    zeros_h = jnp.zeros((n, half), F32)
    cos_t = jnp.concatenate([cos, cos, jnp.ones((n, rest), F32)], axis=1)
    sa_t = jnp.concatenate([-sin, zeros_h, jnp.zeros((n, rest), F32)], axis=1)
    sb_t = jnp.concatenate([zeros_h, sin, jnp.zeros((n, rest), F32)], axis=1)
    reps = LANES // HEAD_DIM
    return tuple(jnp.tile(a, (1, reps)) for a in (cos_t, sa_t, sb_t))


def _head_pool_matrices():
    head = jnp.arange(QK_DIM, dtype=jnp.int32) // HEAD_DIM
    onehot = head[:, None] == jnp.arange(LANES, dtype=jnp.int32)[None, :]
    e = jnp.where(onehot, 1.0 / HEAD_DIM, 0.0).astype(BF16)
    et = jnp.where(onehot.T, 1.0, 0.0).astype(BF16)
    return e, et


def kernel(x_prompt, x_sample, state_conv, cache_k, cache_v, norm_mix_g, w_in, w_dw, b_dw, conv_ln_g,
           conv_ln_b, w_conv_out, q_norm_g, k_norm_g, sinks, w_attn_out, w_out, norm_ffn_g, w_gate_up, w_down):
    nb, t = x_sample.shape[0], x_sample.shape[1]
    seq = x_prompt.shape[1]

    def cut(w, a, b):
        return w[:, :, a:b].astype(BF16)

    c1 = CONV_DIM
    c2 = c1 + CONV_DIM
    c3 = c2 + QK_DIM + KV_DIM
    c4 = c3 + D_MODEL
    w_lin, w_gate, w_qkv = cut(w_in, 0, c1), cut(w_in, c1, c2), cut(w_in, c2, c3)
    w_gc, w_ga = cut(w_in, c3, c4), cut(w_in, c4, c4 + D_MODEL)
    w_ffg, w_ffu = cut(w_gate_up, 0, D_FF), cut(w_gate_up, D_FF, 2 * D_FF)
    w_co, w_ao, w_o, w_dn = (w.astype(BF16) for w in (w_conv_out, w_attn_out, w_out, w_down))

    g_mix = norm_mix_g.reshape(DEPTH, 1, D_MODEL)
    g_ffn = norm_ffn_g.reshape(DEPTH, 1, D_MODEL)
    b_conv = b_dw.reshape(DEPTH, 1, CONV_DIM)
    ln_g = conv_ln_g.reshape(DEPTH, 1, CONV_DIM)
    ln_b = conv_ln_b.reshape(DEPTH, 1, CONV_DIM)
    qk_w = jnp.concatenate([jnp.tile(q_norm_g, (1, N_Q_HEADS)), jnp.tile(k_norm_g, (1, N_KV_HEADS))], axis=1)
    qk_w = qk_w.reshape(DEPTH, 1, QK_DIM)
    e, et = _head_pool_matrices()

    tabs_p = _rope_tables(jnp.arange(seq, dtype=F32))
    tabs_s = _rope_tables(jnp.tile(jnp.arange(t, dtype=F32) + PAST_LEN, nb))
    ck = cache_k.reshape(DEPTH, nb, WINDOW, KV_DIM)
    cv = cache_v.reshape(DEPTH, nb, WINDOW, KV_DIM)

    def token_tail(x, hb, c, o, l):
        mix = _merge(hb, c, o, w_gc, w_ga, w_co, w_ao, l)
        x1, hb2 = _outproj(x, mix, w_o, g_ffn, l)
        out = _ffn(hb2, x1, g_mix, w_ffg, w_ffu, w_dn, l, l + 1 < DEPTH)
        return (out[0], out[1]) if l + 1 < DEPTH else (out[0], None)

    xp = x_prompt.reshape(seq, D_MODEL)
    xs = x_sample.reshape(nb * t, D_MODEL)
    hbp = _norm(xp, g_mix, 0)
    hbs = _norm(xs, g_mix, 0)
    hist_p, k_p, v_p, hist_s, k_s, v_s = [], [], [], [], [], []
    for l in range(DEPTH):
        u3 = _glu(hbp, w_lin, w_gate, l)
        q, k, v = _qkv(hbp, w_qkv, tabs_p, qk_w, e, et, l)
        c = _conv(u3, w_dw, b_conv, ln_g, ln_b, l)
        o = _attn(q, k, v, sinks[l])
        xp, hbp_next = token_tail(xp, hbp, c, o, l)
        hist_p.append(u3[:, seq - (CONV_WIDTH - 1):, :].transpose(1, 0, 2).reshape(1, CONV_WIDTH - 1, CONV_DIM))
        k_p.append(k[seq - WINDOW:].reshape(1, WINDOW, N_KV_HEADS, HEAD_DIM))
        v_p.append(v[seq - WINDOW:].reshape(1, WINDOW, N_KV_HEADS, HEAD_DIM))
        hbp = hbp_next

        u3 = _glu(hbs, w_lin, w_gate, l)
        q, k, v = _qkv(hbs, w_qkv, tabs_s, qk_w, e, et, l)
        u = u3.transpose(1, 0, 2).reshape(nb, t, CONV_DIM)
        c, nh = _conv_s(u, state_conv, w_dw, b_conv, ln_g, ln_b, l)
        o, nk, nv = _attn_s(q, k, v, ck, cv, sinks[l], l, t)
        xs, hbs_next = token_tail(xs, hbs, c.reshape(nb * t, CONV_DIM), o, l)
        hist_s.append(nh)
        k_s.append(nk.reshape(nb, WINDOW, N_KV_HEADS, HEAD_DIM))
        v_s.append(nv.reshape(nb, WINDOW, N_KV_HEADS, HEAD_DIM))
        hbs = hbs_next

    return (xp.reshape(x_prompt.shape), xs.reshape(x_sample.shape),
            jnp.stack(hist_p), jnp.stack(k_p), jnp.stack(v_p),
            jnp.stack(hist_s), jnp.stack(k_s), jnp.stack(v_s))
```
